```python
import math
import jax
import jax.numpy as jnp
from jax import lax
import numpy as np

D_MODEL = 2048
BATCH = 4
SEQ = 4096
DEPTH = 2

A_HEADS = 8
A_HEAD_DIM = 64
A_WIDTH = A_HEADS * A_HEAD_DIM
A_DECAY_LORA = 32
A_ICLR_LORA = 32
A_GATE_LORA = 96
A_GN_EPS = 64e-5
B_HEADS = 4
B_KEY_DIM = 64
B_VAL_DIM = 128
B_KEY_WIDTH = B_HEADS * B_KEY_DIM
B_WIDTH = B_HEADS * B_VAL_DIM
B_GATE_LORA = 16
B_GATE_NORMALIZER = 16.0
B_CHUNK = 64
C_HEADS = 4
C_HEAD_DIM = 128
C_WIDTH = C_HEADS * C_HEAD_DIM
C_CONV = 4
C_CHUNK = 64
N_BRANCH = 3
BRANCH_WIDTH = 512
D_FF = 5632
FFN_CONV = 3
NORM_EPS = 1e-5
L2_EPS = 1e-6
DEEPNORM_ALPHA = (2 * DEPTH) ** 0.25
DEEPNORM_BETA = (8 * DEPTH) ** -0.25

A_SPLITS = (A_WIDTH, A_WIDTH, A_WIDTH, A_DECAY_LORA, A_ICLR_LORA, A_GATE_LORA)
B_SPLITS = (B_KEY_WIDTH, B_KEY_WIDTH, B_WIDTH, B_GATE_LORA, B_WIDTH)
C_SPLITS = (3 * C_WIDTH, C_HEADS, C_HEADS, C_WIDTH)
A_IN = 3 * A_WIDTH + A_DECAY_LORA + A_ICLR_LORA + A_GATE_LORA
B_IN = 2 * B_KEY_WIDTH + B_WIDTH + B_GATE_LORA + B_WIDTH
C_IN = 3 * C_WIDTH + 2 * C_HEADS + C_WIDTH
N_IN = A_IN + B_IN + C_IN

kernel_name = "hybrid_rwkv7_gla_gdn_convffn_deepnorm"

F32 = jnp.float32


def split_cols(p, sizes):
    return jnp.split(p, [int(i) for i in np.cumsum(sizes)[:-1]], axis=-1)


def heads(t, n):
    return t.reshape(t.shape[:-1] + (n, t.shape[-1] // n))


def layer_norm(x, g, b):
    xf = x.astype(F32)
    mu = jnp.mean(xf, -1, keepdims=True)
    var = jnp.mean(jnp.square(xf - mu), -1, keepdims=True)
    return ((xf - mu) * lax.rsqrt(var + NORM_EPS)).astype(x.dtype) * g + b


def rms_norm(x, g):
    xf = x.astype(F32)
    return xf * lax.rsqrt(jnp.mean(jnp.square(xf), -1, keepdims=True) + NORM_EPS) * g


def l2norm(x):
    xf = x.astype(F32)
    return xf * lax.rsqrt(jnp.sum(jnp.square(xf), -1, keepdims=True) + L2_EPS)


def token_shift(x):
    return jnp.pad(x, ((0, 0), (1, 0), (0, 0)))[:, :-1]


def causal_dwconv(x, w):
    k = w.shape[0]
    return lax.conv_general_dilated(
        x, w[:, None, :], window_strides=(1,), padding=[(k - 1, 0)],
        dimension_numbers=('NWC', 'WIO', 'NWC'), feature_group_count=x.shape[-1])


def to_chunks(t, c):
    b, s, h, d = t.shape
    return t.reshape(b, s // c, c, h, d).transpose(0, 3, 1, 2, 4)


def to_chunks_scalar(t, c):
    b, s, h = t.shape
    return t.reshape(b, s // c, c, h).transpose(0, 3, 1, 2)


def from_chunks(t):
    b, h, n, c, d = t.shape
    return t.transpose(0, 2, 3, 1, 4).reshape(b, n * c, h, d)


def rwkv7_branch(pa, w0, w2, a0, a2, g2, k_k, k_a, r_k, lnx_w, lnx_b):
    bsz, seq = pa.shape[0], pa.shape[1]
    r, k, v, wl, al, gl = split_cols(pa, A_SPLITS)
    w = -jax.nn.softplus(-(w0 + jnp.tanh(wl) @ w2)) - 0.5
    decay = jnp.exp(-jnp.exp(w.astype(F32)))
    a = jax.nn.sigmoid(a0 + al @ a2)
    g = jax.nn.sigmoid(gl) @ g2
    kk = l2norm(heads(k * k_k, A_HEADS))
    k = k * (1 + (a - 1) * k_a)
    rh, kh, vh = heads(r, A_HEADS), heads(k, A_HEADS), heads(v, A_HEADS)
    ah, dh = heads(a, A_HEADS), heads(decay, A_HEADS)

    def step(state, inp):
        r_t, w_t, k_t, v_t, kk_t, a_t = inp
        sa = jnp.einsum('bhvk,bhk->bhv', state, -kk_t)
        state = (state * w_t[:, :, None, :]
                 + sa[..., None] * (kk_t * a_t)[:, :, None, :]
                 + v_t[..., None] * k_t[:, :, None, :])
        return state, jnp.einsum('bhvk,bhk->bhv', state, r_t)

    xs = tuple(jnp.moveaxis(t, 1, 0) for t in (rh, dh, kh, vh, kk, ah))
    state0 = jnp.zeros((bsz, A_HEADS, A_HEAD_DIM, A_HEAD_DIM), F32)
    _, o = lax.scan(step, state0, xs)
    o = jnp.moveaxis(o, 0, 1)
    mu = jnp.mean(o, -1, keepdims=True)
    var = jnp.mean(jnp.square(o - mu), -1, keepdims=True)
    on = ((o - mu) * lax.rsqrt(var + A_GN_EPS)).reshape(bsz, seq, A_WIDTH) * lnx_w + lnx_b
    bonus = (jnp.sum(rh * kh * r_k, -1, keepdims=True) * vh).reshape(bsz, seq, A_WIDTH)
    return ((on + bonus) * g).astype(pa.dtype)


def gla_chunked(q, k, v, gk):
    qc, kc, vc = (to_chunks(t.astype(F32), B_CHUNK) for t in (q, k, v))
    bcum = jnp.cumsum(to_chunks(gk, B_CHUNK), axis=-2)
    qd = qc * jnp.exp(bcum)
    kd = kc * jnp.exp(-bcum)
    causal = jnp.tril(jnp.ones((B_CHUNK, B_CHUNK), bool))
    att = jnp.where(causal, jnp.einsum('bhncd,bhnsd->bhncs', qd, kd), 0.0)
    intra = jnp.einsum('bhncs,bhnse->bhnce', att, vc)
    b_last = bcum[..., -1:, :]
    upd = jnp.einsum('bhncd,bhnce->bhnde', kc * jnp.exp(b_last - bcum), vc)
    chunk_decay = jnp.exp(b_last[..., 0, :])

    def step(state, inp):
        dec, u = inp
        return state * dec[..., None] + u, state

    bsz = q.shape[0]
    state0 = jnp.zeros((bsz, B_HEADS, B_KEY_DIM, B_VAL_DIM), F32)
    _, s_prev = lax.scan(step, state0, (jnp.moveaxis(chunk_decay, 2, 0), jnp.moveaxis(upd, 2, 0)))
    s_prev = jnp.moveaxis(s_prev, 0, 2)
    inter = jnp.einsum('bhncd,bhnde->bhnce', qd, s_prev)
    return from_chunks(intra + inter)


def gla_branch(pb, gk_w2, gk_b, norm_w):
    bsz, seq = pb.shape[0], pb.shape[1]
    q, k, v, gkl, g = split_cols(pb, B_SPLITS)
    gk = jax.nn.log_sigmoid((gkl @ gk_w2 + gk_b).astype(F32)) / B_GATE_NORMALIZER
    o = gla_chunked(heads(q, B_HEADS) * B_KEY_DIM ** -0.5, heads(k, B_HEADS),
                    heads(v, B_HEADS), heads(gk, B_HEADS))
    o = rms_norm(o, norm_w).reshape(bsz, seq, B_WIDTH)
    return (o * jax.nn.silu(g)).astype(pb.dtype)


def gated_delta_chunked(q, k, v, g, beta):
    qc, kc, vc = (to_chunks(t.astype(F32), C_CHUNK) for t in (q, k, v))
    bc = to_chunks_scalar(beta, C_CHUNK)
    gam = jnp.cumsum(to_chunks_scalar(g, C_CHUNK), axis=-1)
    diff = gam[..., :, None] - gam[..., None, :]
    incl = jnp.tril(jnp.ones((C_CHUNK, C_CHUNK), bool))
    strict = jnp.tril(jnp.ones((C_CHUNK, C_CHUNK), bool), -1)
    dec_incl = jnp.exp(jnp.where(incl, diff, -jnp.inf))
    dec_strict = jnp.exp(jnp.where(strict, diff, -jnp.inf))
    kb = kc * bc[..., None]
    m = jnp.einsum('bhnid,bhnjd->bhnij', kb, kc) * dec_strict
    eye = jnp.eye(C_CHUNK, dtype=F32)
    rhs = jnp.concatenate([vc * bc[..., None], kb * jnp.exp(gam)[..., None]], axis=-1)
    sol = lax.linalg.triangular_solve(eye + m, rhs, left_side=True, lower=True)
    u, w = sol[..., :C_HEAD_DIM], sol[..., C_HEAD_DIM:]
    qk = jnp.einsum('bhnid,bhnjd->bhnij', qc, kc) * dec_incl
    qg = qc * jnp.exp(gam)[..., None]
    kg = kc * jnp.exp(gam[..., -1:] - gam)[..., None]
    last = jnp.exp(gam[..., -1])

    def step(state, inp):
        u_i, w_i, qk_i, qg_i, kg_i, last_i = inp
        v_new = u_i - jnp.einsum('bhcd,bhde->bhce', w_i, state)
        o = (jnp.einsum('bhcd,bhde->bhce', qg_i, state)
             + jnp.einsum('bhcs,bhse->bhce', qk_i, v_new))
        state = state * last_i[..., None, None] + jnp.einsum('bhcd,bhce->bhde', kg_i, v_new)
        return state, o

    bsz = q.shape[0]
    xs = tuple(jnp.moveaxis(t, 2, 0) for t in (u, w, qk, qg, kg, last))
    state0 = jnp.zeros((bsz, C_HEADS, C_HEAD_DIM, C_HEAD_DIM), F32)
    _, o = lax.scan(step, state0, xs)
    return from_chunks(jnp.moveaxis(o, 0, 2))


def gdn_branch(pc, conv_w, a_log, dt_bias, norm_w):
    bsz, seq = pc.shape[0], pc.shape[1]
    qkv, a, b, z = split_cols(pc, C_SPLITS)
    qkv = jax.nn.silu(causal_dwconv(qkv, conv_w))
    q, k, v = jnp.split(qkv, 3, axis=-1)
    q = l2norm(heads(q, C_HEADS)) * C_HEAD_DIM ** -0.5
    k = l2norm(heads(k, C_HEADS))
    g = -jnp.exp(a_log.astype(F32)) * jax.nn.softplus(a.astype(F32) + dt_bias)
    beta = jax.nn.sigmoid(b.astype(F32))
    o = gated_delta_chunked(q, k, heads(v, C_HEADS), g, beta)
    o = rms_norm(o, norm_w) * jax.nn.silu(heads(z, C_HEADS).astype(F32))
    return o.reshape(bsz, seq, C_WIDTH).astype(pc.dtype)


def setup_inputs(seed: int = 0) -> dict:
    key = jax.random.key(seed)
    ks = iter(jax.random.split(key, 40))

    def nrm(shape, scale):
        return scale * jax.random.normal(next(ks), shape, F32)

    def unif(shape, lo, hi):
        return jax.random.uniform(next(ks), shape, F32, lo, hi)

    L, D = DEPTH, D_MODEL
    dt = jnp.exp(unif((L, C_HEADS), math.log(1e-3), math.log(1e-1)))
    return {
        'x': nrm((BATCH, SEQ, D), 1.0),
        'ln_in_g': 1.0 + nrm((D,), 0.02),
        'ln_in_b': nrm((D,), 0.02),
        'w_in': nrm((L, D, N_IN), D ** -0.5),
        'mu_a': unif((L, A_IN), 0.0, 1.0),
        'a_w0': -1.0 + nrm((L, A_WIDTH), 0.5),
        'a_w2': nrm((L, A_DECAY_LORA, A_WIDTH), 0.5 * A_DECAY_LORA ** -0.5),
        'a_a0': nrm((L, A_WIDTH), 0.5),
        'a_a2': nrm((L, A_ICLR_LORA, A_WIDTH), 0.5 * A_ICLR_LORA ** -0.5),
        'a_g2': nrm((L, A_GATE_LORA, A_WIDTH), A_GATE_LORA ** -0.5),
        'a_kk': 0.85 + nrm((L, A_WIDTH), 0.05),
        'a_ka': 1.0 + nrm((L, A_WIDTH), 0.05),
        'a_rk': nrm((L, A_HEADS, A_HEAD_DIM), 0.1),
        'a_lnx_w': 1.0 + nrm((L, A_WIDTH), 0.02),
        'a_lnx_b': nrm((L, A_WIDTH), 0.02),
        'b_gk_w2': nrm((L, B_GATE_LORA, B_KEY_WIDTH), B_GATE_LORA ** -0.5),
        'b_gk_b': nrm((L, B_KEY_WIDTH), 0.1),
        'b_norm_w': 1.0 + nrm((L, B_VAL_DIM), 0.02),
        'c_conv_w': nrm((L, C_CONV, 3 * C_WIDTH), C_CONV ** -0.5),
        'c_a_log': jnp.log(unif((L, C_HEADS), 1.0, 16.0)),
        'c_dt_bias': dt + jnp.log(-jnp.expm1(-dt)),
        'c_norm_w': 1.0 + nrm((L, C_HEAD_DIM), 0.02),
        'w_gate': nrm((L, N_BRANCH, D, D), D ** -0.5),
        'w_branch': nrm((L, N_BRANCH, BRANCH_WIDTH, D), BRANCH_WIDTH ** -0.5),
        'w_out': nrm((L, D, D), DEEPNORM_BETA * D ** -0.5),
        'ln1_g': 1.0 + nrm((L, D), 0.02),
        'ln1_b': nrm((L, D), 0.02),
        'w_up': nrm((L, D, 2 * D_FF), D ** -0.5),
        'ffn_conv_w': nrm((L, FFN_CONV, 2 * D_FF), FFN_CONV ** -0.5),
        'ffn_conv_b': nrm((L, 2 * D_FF), 0.02),
        'w_down': nrm((L, D_FF, D), DEEPNORM_BETA * D_FF ** -0.5),
        'ln2_g': 1.0 + nrm((L, D), 0.02),
        'ln2_b': nrm((L, D), 0.02),
    }


def reference(x, ln_in_g, ln_in_b, w_in, mu_a, a_w0, a_w2, a_a0, a_a2, a_g2, a_kk, a_ka,
              a_rk, a_lnx_w, a_lnx_b, b_gk_w2, b_gk_b, b_norm_w, c_conv_w, c_a_log,
              c_dt_bias, c_norm_w, w_gate, w_branch, w_out, ln1_g, ln1_b, w_up,
              ffn_conv_w, ffn_conv_b, w_down, ln2_g, ln2_b):
    h = layer_norm(x, ln_in_g, ln_in_b)
    for l in range(DEPTH):
        p = h @ w_in[l]
        pa, pb, pc = split_cols(p, (A_IN, B_IN, C_IN))
        pa = pa + (token_shift(pa) - pa) * mu_a[l]
        ya = rwkv7_branch(pa, a_w0[l], a_w2[l], a_a0[l], a_a2[l], a_g2[l], a_kk[l],
                          a_ka[l], a_rk[l], a_lnx_w[l], a_lnx_b[l])
        yb = gla_branch(pb, b_gk_w2[l], b_gk_b[l], b_norm_w[l])
        yc = gdn_branch(pc, c_conv_w[l], c_a_log[l], c_dt_bias[l], c_norm_w[l])
        merged = jnp.zeros_like(h)
        for i, y in enumerate((ya, yb, yc)):
            merged = merged + jax.nn.sigmoid(h @ w_gate[l, i]) * (y @ w_branch[l, i])
        h = layer_norm(DEEPNORM_ALPHA * h + merged @ w_out[l], ln1_g[l], ln1_b[l])
        u = causal_dwconv(h @ w_up[l], ffn_conv_w[l]) + ffn_conv_b[l]
        gate, up = jnp.split(u, 2, axis=-1)
        h = layer_norm(DEEPNORM_ALPHA * h + (jax.nn.silu(gate) * up) @ w_down[l],
                       ln2_g[l], ln2_b[l])
    return h
```

```python
import functools
import math

import jax
import jax.numpy as jnp
from jax import lax
from jax.experimental import pallas as pl
from jax.experimental.pallas import tpu as pltpu

F32 = jnp.float32
BF16 = jnp.bfloat16
HI = lax.Precision.HIGHEST

DEPTH = 2
A_HEADS, A_DIM, A_WIDTH = 8, 64, 512
A_DECAY_LORA, A_ICLR_LORA, A_GATE_LORA = 32, 32, 96
A_GN_EPS = 64e-5
B_HEADS, B_KEY, B_VAL = 4, 64, 128
B_KEY_WIDTH, B_WIDTH = 256, 512
B_GATE_LORA = 16
B_GATE_NORMALIZER = 16.0
C_HEADS, C_DIM, C_WIDTH = 4, 128, 512
C_CONV = 4
D_FF = 5632
FFN_CONV = 3
NORM_EPS = 1e-5
L2_EPS = 1e-6
DEEPNORM_ALPHA = (2 * DEPTH) ** 0.25

CHUNK = 64
HALO = 8

P_A_MAIN = 0
P_B_MAIN = 1536
P_C_QKV = 3072
P_C_Z = 4608
P_A_SMALL = 5120
P_B_SMALL = 5376
P_C_SMALL = 5504
P_WIDTH = 5632

VMEM_LIMIT = 56 * 1024 * 1024


def _cparams(sem):
    return pltpu.CompilerParams(dimension_semantics=sem, vmem_limit_bytes=VMEM_LIMIT)


def _dot(a, b, prec=None):
    return jnp.dot(a, b, preferred_element_type=F32, precision=prec)


def _dot_nt(a, b, prec=None):
    return lax.dot_general(a, b, (((1,), (1,)), ((), ())), preferred_element_type=F32, precision=prec)


def _dot_tn(a, b, prec=None):
    return lax.dot_general(a, b, (((0,), (0,)), ((), ())), preferred_element_type=F32, precision=prec)


def _layer_norm(x, g, b):
    mu = jnp.mean(x, -1, keepdims=True)
    xc = x - mu
    var = jnp.mean(xc * xc, -1, keepdims=True)
    return xc * lax.rsqrt(var + NORM_EPS) * g + b


def _softplus(x):
    return jnp.maximum(x, 0.0) + jnp.log1p(jnp.exp(-jnp.abs(x)))


def _sigmoid(x):
    return jax.nn.sigmoid(x)


def _silu(x):
    return x * jax.nn.sigmoid(x)


def _tri_masks(n):
    row = lax.broadcasted_iota(jnp.int32, (n, n), 0)
    col = lax.broadcasted_iota(jnp.int32, (n, n), 1)
    return row, col


def _tri_inv(m, row, col):
    n = m.shape[0]
    eye = (row == col).astype(F32)
    same16 = (row // 16) == (col // 16)
    same32 = (row // 32) == (col // 32)
    m16 = jnp.where(same16, m, 0.0)
    x = eye - m16
    p = _dot(m16, m16, HI)
    x = x + _dot(x, p, HI)
    p = _dot(p, p, HI)
    x = x + _dot(x, p, HI)
    p = _dot(p, p, HI)
    x = x + _dot(x, p, HI)
    o32 = jnp.where(same32 & jnp.logical_not(same16), m, 0.0)
    x = x - _dot(x, _dot(o32, x, HI), HI)
    if n > 32:
        o64 = jnp.where(same32, 0.0, m)
        x = x - _dot(x, _dot(o64, x, HI), HI)
    return x


def _ln_kernel(x_ref, g_ref, b_ref, o_ref, ob_ref):
    y = _layer_norm(x_ref[...], g_ref[...], b_ref[...])
    o_ref[...] = y
    ob_ref[...] = y.astype(BF16)


def _ln_in(x, g, b, tm=256):
    m, d = x.shape
    return pl.pallas_call(
        _ln_kernel,
        grid=(m // tm,),
        in_specs=[pl.BlockSpec((tm, d), lambda i: (i, 0)),
                  pl.BlockSpec((1, d), lambda i: (0, 0)),
                  pl.BlockSpec((1, d), lambda i: (0, 0))],
        out_specs=[pl.BlockSpec((tm, d), lambda i: (i, 0)),
                   pl.BlockSpec((tm, d), lambda i: (i, 0))],
        out_shape=[jax.ShapeDtypeStruct((m, d), F32), jax.ShapeDtypeStruct((m, d), BF16)],
        compiler_params=_cparams(("parallel",)),
        name="ln_in",
    )(x, g.reshape(1, d), b.reshape(1, d))


def _mm_kernel(a_ref, w_ref, o_ref):
    o_ref[...] = _dot(a_ref[...], w_ref[...]).astype(o_ref.dtype)


def _matmul(a, w, tm, tn, out_dtype, name):
    m, k = a.shape
    n = w.shape[1]
    tm = min(tm, m)
    return pl.pallas_call(
        _mm_kernel,
        grid=(m // tm, n // tn),
        in_specs=[pl.BlockSpec((tm, k), lambda i, j: (i, 0)),
                  pl.BlockSpec((k, tn), lambda i, j: (0, j))],
        out_specs=pl.BlockSpec((tm, tn), lambda i, j: (i, j)),
        out_shape=jax.ShapeDtypeStruct((m, n), out_dtype),
        compiler_params=_cparams(("parallel", "parallel")),
        name=name,
    )(a, w)


def _rwkv_kernel(main_ref, small_ref, mu_main_ref, mu_small_ref, w0_ref, w2_ref, a0_ref, a2_ref, g2_ref,
                 kk_ref, ka_ref, rk_ref, lnw_ref, lnb_ref, o_ref, xbuf, state):
    c = pl.program_id(1)
    n = CHUNK
    wm = 3 * A_WIDTH

    @pl.when(c == 0)
    def _():
        xbuf[0:HALO, :] = jnp.zeros((HALO, xbuf.shape[1]), F32)
        state[...] = jnp.zeros(state.shape, F32)

    raw_main = main_ref[...]
    raw_small = small_ref[...]
    xbuf[HALO:HALO + n, 0:wm] = raw_main
    xbuf[HALO:HALO + n, wm:] = raw_small
    prev = xbuf[HALO - 1:HALO - 1 + n, :]
    xbuf[0:HALO, :] = xbuf[n:n + HALO, :]
    x_main = raw_main + (prev[:, 0:wm] - raw_main) * mu_main_ref[...]
    x_small = raw_small + (prev[:, wm:] - raw_small) * mu_small_ref[...]

    r = x_main[:, 0:A_WIDTH]
    k = x_main[:, A_WIDTH:2 * A_WIDTH]
    v = x_main[:, 2 * A_WIDTH:3 * A_WIDTH]
    wlog = -_softplus(-(w0_ref[...] + _dot(jnp.tanh(x_small), w2_ref[...], HI))) - 0.5
    logdec = -jnp.exp(wlog)
    a = _sigmoid(a0_ref[...] + _dot(x_small, a2_ref[...], HI))
    g = _dot(_sigmoid(x_small), g2_ref[...], HI)
    kkr = k * kk_ref[...]
    k = k * (1.0 + (a - 1.0) * ka_ref[...])
    rkk = r * k * rk_ref[...]

    row, col = _tri_masks(n)
    incl = row >= col
    strict = row > col
    cum = _dot(incl.astype(F32), logdec, HI)
    e_incl = jnp.exp(cum)
    e_excl = jnp.exp(cum - logdec)
    e_neg = jnp.exp(-cum)
    e_rest = jnp.exp(cum[n - 1:n, :] - cum)
    e_last = e_incl[n - 1:n, :]

    outs = []
    for h in range(A_HEADS):
        sl = slice(h * A_DIM, (h + 1) * A_DIM)
        r_h, k_h, v_h, a_h = r[:, sl], k[:, sl], v[:, sl], a[:, sl]
        kk_h = kkr[:, sl]
        kk_h = kk_h * lax.rsqrt(jnp.sum(kk_h * kk_h, -1, keepdims=True) + L2_EPS)
        b_h = kk_h * a_h
        lhs = jnp.concatenate([-kk_h * e_excl[:, sl], r_h * e_incl[:, sl]], axis=0)
        bt = b_h * e_neg[:, sl]
        kt = k_h * e_neg[:, sl]
        pb = _dot_nt(lhs, bt, HI)
        pk = _dot_nt(lhs, kt, HI)
        s0 = state[h]
        ls = _dot_nt(lhs, s0, HI)
        ab = jnp.where(strict, pb[0:n], 0.0)
        ak = jnp.where(strict, pk[0:n], 0.0)
        rb = jnp.where(incl, pb[n:2 * n], 0.0)
        rkm = jnp.where(incl, pk[n:2 * n], 0.0)
        u = _dot(_tri_inv(-ab, row, col), ls[0:n] + _dot(ak, v_h, HI), HI)
        o = ls[n:2 * n] + _dot(rb, u, HI) + _dot(rkm, v_h, HI)
        state[h] = (s0 * e_last[:, sl] + _dot_tn(u, b_h * e_rest[:, sl], HI)
                    + _dot_tn(v_h, k_h * e_rest[:, sl], HI))
        mu = jnp.mean(o, -1, keepdims=True)
        oc = o - mu
        var = jnp.mean(oc * oc, -1, keepdims=True)
        on = oc * lax.rsqrt(var + A_GN_EPS)
        bonus = jnp.sum(rkk[:, sl], -1, keepdims=True) * v_h
        outs.append((on, bonus))
    on = jnp.concatenate([t[0] for t in outs], axis=1)
    bonus = jnp.concatenate([t[1] for t in outs], axis=1)
    o_ref[...] = ((on * lnw_ref[...] + lnb_ref[...] + bonus) * g).astype(o_ref.dtype)


def _rwkv(p, bsz, seq, mu_main, mu_small, w0, w2p, a0, a2p, g2p, kk, ka, rk, lnw, lnb):
    nc = seq // CHUNK
    n = CHUNK
    row = lambda width, blk: pl.BlockSpec((n, width), lambda b, c: (b * nc + c, blk))
    full = lambda arr: pl.BlockSpec(arr.shape, lambda b, c: (0,) * arr.ndim)
    params = (mu_main, mu_small, w0, w2p, a0, a2p, g2p, kk, ka, rk, lnw, lnb)
    return pl.pallas_call(
        _rwkv_kernel,
        grid=(bsz, nc),
        in_specs=[row(3 * A_WIDTH, P_A_MAIN // (3 * A_WIDTH)), row(256, P_A_SMALL // 256)]
                 + [full(t) for t in params],
        out_specs=pl.BlockSpec((n, A_WIDTH), lambda b, c: (b * nc + c, 0)),
        out_shape=jax.ShapeDtypeStruct((bsz * seq, A_WIDTH), BF16),
        scratch_shapes=[pltpu.VMEM((n + HALO, 3 * A_WIDTH + 256), F32),
                        pltpu.VMEM((A_HEADS, A_DIM, A_DIM), F32)],
        compiler_params=_cparams(("arbitrary", "arbitrary")),
        name="rwkv7",
    )(p, p, *params)


def _gla_kernel(main_ref, small_ref, gkw_ref, gkb_ref, nw_ref, o_ref, state):
    c = pl.program_id(1)
    n = CHUNK

    @pl.when(c == 0)
    def _():
        state[...] = jnp.zeros(state.shape, F32)

    x = main_ref[...]
    q = x[:, 0:B_KEY_WIDTH] * (B_KEY ** -0.5)
    k = x[:, B_KEY_WIDTH:2 * B_KEY_WIDTH]
    v = x[:, 2 * B_KEY_WIDTH:2 * B_KEY_WIDTH + B_WIDTH]
    g = x[:, 2 * B_KEY_WIDTH + B_WIDTH:]
    gk = -_softplus(-(_dot(small_ref[...], gkw_ref[...], HI) + gkb_ref[...])) / B_GATE_NORMALIZER
    row, col = _tri_masks(n)
    incl = row >= col
    bcum = _dot(incl.astype(F32), gk, HI)
    e_pos = jnp.exp(bcum)
    e_neg = jnp.exp(-bcum)
    e_rest = jnp.exp(bcum[n - 1:n, :] - bcum)
    e_last = e_pos[n - 1:n, :]
    outs = []
    for h in range(B_HEADS):
        ks = slice(h * B_KEY, (h + 1) * B_KEY)
        vs = slice(h * B_VAL, (h + 1) * B_VAL)
        qd = q[:, ks] * e_pos[:, ks]
        kd = k[:, ks] * e_neg[:, ks]
        v_h = v[:, vs]
        att = jnp.where(incl, _dot_nt(qd, kd, HI), 0.0)
        s0 = state[h]
        o = _dot(att, v_h, HI) + _dot_nt(qd, s0, HI)
        state[h] = s0 * e_last[:, ks] + _dot_tn(v_h, k[:, ks] * e_rest[:, ks], HI)
        o = o * lax.rsqrt(jnp.mean(o * o, -1, keepdims=True) + NORM_EPS) * nw_ref[...]
        outs.append(o)
    o = jnp.concatenate(outs, axis=1)
    o_ref[...] = (o * _silu(g)).astype(o_ref.dtype)


def _gla(p, bsz, seq, gkw_p, gkb, nw):
    nc = seq // CHUNK
    n = CHUNK
    full = lambda arr: pl.BlockSpec(arr.shape, lambda b, c: (0,) * arr.ndim)
    params = (gkw_p, gkb, nw)
    return pl.pallas_call(
        _gla_kernel,
        grid=(bsz, nc),
        in_specs=[pl.BlockSpec((n, 1536), lambda b, c: (b * nc + c, P_B_MAIN // 1536)),
                  pl.BlockSpec((n, 128), lambda b, c: (b * nc + c, P_B_SMALL // 128))]
                 + [full(t) for t in params],
        out_specs=pl.BlockSpec((n, B_WIDTH), lambda b, c: (b * nc + c, 0)),
        out_shape=jax.ShapeDtypeStruct((bsz * seq, B_WIDTH), BF16),
        scratch_shapes=[pltpu.VMEM((B_HEADS, B_VAL, B_KEY), F32)],
        compiler_params=_cparams(("arbitrary", "arbitrary")),
        name="gla",
    )(p, p, *params)


def _gdn_kernel(qkv_ref, z_ref, small_ref, cw_ref, alog_ref, dtb_ref, nw_ref, o_ref, xbuf, state):
    c = pl.program_id(1)
    n = CHUNK

    @pl.when(c == 0)
    def _():
        xbuf[0:HALO, :] = jnp.zeros((HALO, xbuf.shape[1]), F32)
        state[...] = jnp.zeros(state.shape, F32)

    raw = qkv_ref[...]
    xbuf[HALO:HALO + n, :] = raw
    y = raw * cw_ref[C_CONV - 1:C_CONV, :]
    for j in range(1, C_CONV):
        y = y + xbuf[HALO - j:HALO - j + n, :] * cw_ref[C_CONV - 1 - j:C_CONV - j, :]
    xbuf[0:HALO, :] = xbuf[n:n + HALO, :]
    qkv = _silu(y)
    q = qkv[:, 0:C_WIDTH]
    k = qkv[:, C_WIDTH:2 * C_WIDTH]
    v = qkv[:, 2 * C_WIDTH:]
    z = z_ref[...]
    small = small_ref[...]
    g_all = -jnp.exp(alog_ref[...]) * _softplus(small + dtb_ref[...])
    beta_all = _sigmoid(small)
    row, col = _tri_masks(n)
    incl = row >= col
    strict = row > col
    gam = _dot(incl.astype(F32), g_all, HI)
    gam_t = gam.T
    outs = []
    for h in range(C_HEADS):
        sl = slice(h * C_DIM, (h + 1) * C_DIM)
        q_h, k_h, v_h = q[:, sl], k[:, sl], v[:, sl]
        q_h = q_h * (lax.rsqrt(jnp.sum(q_h * q_h, -1, keepdims=True) + L2_EPS) * (C_DIM ** -0.5))
        k_h = k_h * lax.rsqrt(jnp.sum(k_h * k_h, -1, keepdims=True) + L2_EPS)
        gcol = gam[:, h:h + 1]
        grow = gam_t[h:h + 1, :]
        beta = beta_all[:, C_HEADS + h:C_HEADS + h + 1]
        diff = gcol - grow
        dec = jnp.exp(jnp.where(incl, diff, 0.0))
        dec_incl = jnp.where(incl, dec, 0.0)
        dec_strict = jnp.where(strict, dec, 0.0)
        kb = k_h * beta
        m = _dot_nt(kb, k_h, HI) * dec_strict
        tinv = _tri_inv(m, row, col)
        eg = jnp.exp(gcol)
        glast = gam[n - 1:n, h:h + 1]
        u = _dot(tinv, v_h * beta, HI)
        w = _dot(tinv, kb * eg, HI)
        qk = _dot_nt(q_h, k_h, HI) * dec_incl
        s0 = state[h]
        v_new = u - _dot(w, s0, HI)
        o = _dot(q_h * eg, s0, HI) + _dot(qk, v_new, HI)
        state[h] = s0 * jnp.exp(glast) + _dot_tn(k_h * jnp.exp(glast - gcol), v_new, HI)
        o = o * lax.rsqrt(jnp.mean(o * o, -1, keepdims=True) + NORM_EPS) * nw_ref[...]
        outs.append(o)
    o = jnp.concatenate(outs, axis=1)
    o_ref[...] = (o * _silu(z)).astype(o_ref.dtype)


def _gdn(p, bsz, seq, cw, alog_p, dtb_p, nw):
    nc = seq // CHUNK
    n = CHUNK
    full = lambda arr: pl.BlockSpec(arr.shape, lambda b, c: (0,) * arr.ndim)
    params = (cw, alog_p, dtb_p, nw)
    return pl.pallas_call(
        _gdn_kernel,
        grid=(bsz, nc),
        in_specs=[pl.BlockSpec((n, 1536), lambda b, c: (b * nc + c, P_C_QKV // 1536)),
                  pl.BlockSpec((n, 512), lambda b, c: (b * nc + c, P_C_Z // 512)),
                  pl.BlockSpec((n, 128), lambda b, c: (b * nc + c, P_C_SMALL // 128))]
                 + [full(t) for t in params],
        out_specs=pl.BlockSpec((n, C_WIDTH), lambda b, c: (b * nc + c, 0)),
        out_shape=jax.ShapeDtypeStruct((bsz * seq, C_WIDTH), BF16),
        scratch_shapes=[pltpu.VMEM((n + HALO, 3 * C_WIDTH), F32),
                        pltpu.VMEM((C_HEADS, C_DIM, C_DIM), F32)],
        compiler_params=_cparams(("arbitrary", "arbitrary")),
        name="gdn",
    )(p, p, p, *params)


def _merge_kernel(h_ref, ya_ref, yb_ref, yc_ref, wg_ref, wb_ref, o_ref):
    h = h_ref[...]
    acc = None
    for i, y_ref in enumerate((ya_ref, yb_ref, yc_ref)):
        t = _sigmoid(_dot(h, wg_ref[i])) * _dot(y_ref[...], wb_ref[i])
        acc = t if acc is None else acc + t
    o_ref[...] = acc.astype(o_ref.dtype)


def _merge(hb, ya, yb, yc, wg, wb, tm=1024, tn=256):
    m, d = hb.shape
    bw = ya.shape[1]
    tm = min(tm, m)
    yspec = pl.BlockSpec((tm, bw), lambda i, j: (i, 0))
    return pl.pallas_call(
        _merge_kernel,
        grid=(m // tm, d // tn),
        in_specs=[pl.BlockSpec((tm, d), lambda i, j: (i, 0)), yspec, yspec, yspec,
                  pl.BlockSpec((3, d, tn), lambda i, j: (0, 0, j)),
                  pl.BlockSpec((3, bw, tn), lambda i, j: (0, 0, j))],
        out_specs=pl.BlockSpec((tm, tn), lambda i, j: (i, j)),
        out_shape=jax.ShapeDtypeStruct((m, d), BF16),
        compiler_params=_cparams(("parallel", "parallel")),
        name="merge",
    )(hb, ya, yb, yc, wg, wb)


def _proj_ln_kernel(a_ref, w_ref, h_ref, g_ref, b_ref, o_ref, ob_ref, acc_ref):
    kstep = pl.program_id(1)

    @pl.when(kstep == 0)
    def _():
        acc_ref[...] = jnp.zeros(acc_ref.shape, F32)

    acc_ref[...] += _dot(a_ref[...], w_ref[...])

    @pl.when(kstep == pl.num_programs(1) - 1)
    def _():
        y = _layer_norm(DEEPNORM_ALPHA * h_ref[...] + acc_ref[...], g_ref[...], b_ref[...])
        o_ref[...] = y
        ob_ref[...] = y.astype(BF16)


def _proj_ln(a, w, h, g, b, tm, tk, name):
    m, k = a.shape
    d = w.shape[1]
    tm = min(tm, m)
    return pl.pallas_call(
        _proj_ln_kernel,
        grid=(m // tm, k // tk),
        in_specs=[pl.BlockSpec((tm, tk), lambda i, j: (i, j)),
                  pl.BlockSpec((tk, d), lambda i, j: (j, 0)),
                  pl.BlockSpec((tm, d), lambda i, j: (i, 0)),
                  pl.BlockSpec((1, d), lambda i, j: (0, 0)),
                  pl.BlockSpec((1, d), lambda i, j: (0, 0))],
        out_specs=[pl.BlockSpec((tm, d), lambda i, j: (i, 0)),
                   pl.BlockSpec((tm, d), lambda i, j: (i, 0))],
        out_shape=[jax.ShapeDtypeStruct((m, d), F32), jax.ShapeDtypeStruct((m, d), BF16)],
        scratch_shapes=[pltpu.VMEM((tm, d), F32)],
        compiler_params=_cparams(("parallel", "arbitrary")),
        name=name,
    )(a, w, h, g.reshape(1, d), b.reshape(1, d))


def _ffn_up_kernel(a_ref, wg_ref, wu_ref, cwg_ref, cwu_ref, bg_ref, bu_ref, o_ref, gbuf, ubuf, gcar, ucar,
                   *, blocks_per_seq):
    i = pl.program_id(0)
    j = pl.program_id(1)
    tm = a_ref.shape[0]
    a = a_ref[...]
    first = (i % blocks_per_seq) == 0

    def conv(w_ref, cw_ref, b_ref, buf, car):
        u = _dot(a, w_ref[...])
        buf[HALO:HALO + tm, :] = u
        buf[0:HALO, :] = jnp.where(first, 0.0, car[j])
        car[j] = u[tm - HALO:tm, :]
        return (u * cw_ref[2:3, :] + buf[HALO - 1:HALO - 1 + tm, :] * cw_ref[1:2, :]
                + buf[HALO - 2:HALO - 2 + tm, :] * cw_ref[0:1, :] + b_ref[...])

    gate = conv(wg_ref, cwg_ref, bg_ref, gbuf, gcar)
    up = conv(wu_ref, cwu_ref, bu_ref, ubuf, ucar)
    o_ref[...] = (_silu(gate) * up).astype(o_ref.dtype)


def _ffn_up(hb, w_up, conv_w, conv_b, seq, tm=1024, tn=512):
    m, d = hb.shape
    nf = w_up.shape[1] // 2
    nj = nf // tn
    tm = min(tm, seq)
    kern = functools.partial(_ffn_up_kernel, blocks_per_seq=seq // tm)
    return pl.pallas_call(
        kern,
        grid=(m // tm, nj),
        in_specs=[pl.BlockSpec((tm, d), lambda i, j: (i, 0)),
                  pl.BlockSpec((d, tn), lambda i, j: (0, j)),
                  pl.BlockSpec((d, tn), lambda i, j: (0, j + nj)),
                  pl.BlockSpec((FFN_CONV, tn), lambda i, j: (0, j)),
                  pl.BlockSpec((FFN_CONV, tn), lambda i, j: (0, j + nj)),
                  pl.BlockSpec((1, tn), lambda i, j: (0, j)),
                  pl.BlockSpec((1, tn), lambda i, j: (0, j + nj))],
        out_specs=pl.BlockSpec((tm, tn), lambda i, j: (i, j)),
        out_shape=jax.ShapeDtypeStruct((m, nf), BF16),
        scratch_shapes=[pltpu.VMEM((tm + HALO, tn), F32), pltpu.VMEM((tm + HALO, tn), F32),
                        pltpu.VMEM((nj, HALO, tn), F32), pltpu.VMEM((nj, HALO, tn), F32)],
        compiler_params=_cparams(("arbitrary", "arbitrary")),
        name="ffn_up",
    )(hb, w_up, w_up, conv_w, conv_w, conv_b, conv_b)


def _pad_cols(t, width):
    return jnp.pad(t, ((0, 0), (0, width - t.shape[1])))


def _pad_rows(t, before, total):
    return jnp.pad(t, ((before, total - before - t.shape[0]), (0, 0)))


def _relayout_w_in(w):
    a_in = 3 * A_WIDTH + A_DECAY_LORA + A_ICLR_LORA + A_GATE_LORA
    b_in = 2 * B_KEY_WIDTH + B_WIDTH + B_GATE_LORA + B_WIDTH
    wa, wb, wc = w[:, :a_in], w[:, a_in:a_in + b_in], w[:, a_in + b_in:]
    b_qkv = wb[:, :2 * B_KEY_WIDTH + B_WIDTH]
    b_gk = wb[:, 2 * B_KEY_WIDTH + B_WIDTH:2 * B_KEY_WIDTH + B_WIDTH + B_GATE_LORA]
    b_g = wb[:, 2 * B_KEY_WIDTH + B_WIDTH + B_GATE_LORA:]
    c_qkv = wc[:, :3 * C_WIDTH]
    c_ab = wc[:, 3 * C_WIDTH:3 * C_WIDTH + 2 * C_HEADS]
    c_z = wc[:, 3 * C_WIDTH + 2 * C_HEADS:]
    return jnp.concatenate([
        wa[:, :3 * A_WIDTH], b_qkv, b_g, c_qkv, c_z,
        _pad_cols(wa[:, 3 * A_WIDTH:], 256), _pad_cols(b_gk, 128), _pad_cols(c_ab, 128)], axis=1)


def kernel(x, ln_in_g, ln_in_b, w_in, mu_a, a_w0, a_w2, a_a0, a_a2, a_g2, a_kk, a_ka, a_rk, a_lnx_w, a_lnx_b,
           b_gk_w2, b_gk_b, b_norm_w, c_conv_w, c_a_log, c_dt_bias, c_norm_w, w_gate, w_branch, w_out,
           ln1_g, ln1_b, w_up, ffn_conv_w, ffn_conv_b, w_down, ln2_g, ln2_b):
    bsz, seq, d = x.shape
    m = bsz * seq
    assert seq % CHUNK == 0
    h, hb = _ln_in(x.reshape(m, d), ln_in_g, ln_in_b)
    r1 = lambda t: t.reshape(1, -1)
    for l in range(DEPTH):
        p = _matmul(hb, _relayout_w_in(w_in[l]).astype(BF16), 1024, 512, F32, "in_proj")
        lo = A_DECAY_LORA
        ya = _rwkv(
            p, bsz, seq,
            r1(mu_a[l, :3 * A_WIDTH]), _pad_cols(r1(mu_a[l, 3 * A_WIDTH:]), 256),
            r1(a_w0[l]), _pad_rows(a_w2[l], 0, 256),
            r1(a_a0[l]), _pad_rows(a_a2[l], lo, 256),
            _pad_rows(a_g2[l], lo + A_ICLR_LORA, 256),
            r1(a_kk[l]), r1(a_ka[l]), r1(a_rk[l]), r1(a_lnx_w[l]), r1(a_lnx_b[l]))
        yb = _gla(p, bsz, seq, _pad_rows(b_gk_w2[l], 0, 128), r1(b_gk_b[l]), r1(b_norm_w[l]))
        yc = _gdn(p, bsz, seq, c_conv_w[l], _pad_cols(r1(c_a_log[l]), 128), _pad_cols(r1(c_dt_bias[l]), 128),
                  r1(c_norm_w[l]))
        merged = _merge(hb, ya, yb, yc, w_gate[l].astype(BF16), w_branch[l].astype(BF16))
        h, hb = _proj_ln(merged, w_out[l].astype(BF16), h, ln1_g[l], ln1_b[l], 256, d, "out_proj")
        act = _ffn_up(hb, w_up[l].astype(BF16), ffn_conv_w[l], r1(ffn_conv_b[l]), seq)
        h, hb = _proj_ln(act, w_down[l].astype(BF16), h, ln2_g[l], ln2_b[l], 512, 512, "ffn_down")
    return h.reshape(bsz, seq, d)
```

```python
import functools

import jax
import jax.numpy as jnp
from jax import lax
from jax.experimental import pallas as pl
from jax.experimental.pallas import tpu as pltpu

F32 = jnp.float32
BF16 = jnp.bfloat16
HI = lax.Precision.HIGHEST

DEPTH = 2
A_HEADS, A_DIM, A_WIDTH = 8, 64, 512
A_DECAY_LORA, A_ICLR_LORA, A_GATE_LORA = 32, 32, 96
A_GN_EPS = 64e-5
B_HEADS, B_KEY, B_VAL = 4, 64, 128
B_KEY_WIDTH, B_WIDTH = 256, 512
B_GATE_LORA = 16
B_GATE_NORMALIZER = 16.0
C_HEADS, C_DIM, C_WIDTH = 4, 128, 512
C_CONV = 4
D_FF = 5632
FFN_CONV = 3
NORM_EPS = 1e-5
L2_EPS = 1e-6
DEEPNORM_ALPHA = (2 * DEPTH) ** 0.25

CHUNK = 64
HALO = 8
HALF = 256

P_A_MAIN = 0
P_B_MAIN = 1536
P_C_QKV = 3072
P_C_Z = 4608
P_A_SMALL = 5120
P_B_SMALL = 5376
P_C_SMALL = 5504
P_WIDTH = 5632

VMEM_LIMIT = 56 * 1024 * 1024


def _cparams(sem):
    return pltpu.CompilerParams(dimension_semantics=sem, vmem_limit_bytes=VMEM_LIMIT)


def _dot(a, b, prec=None):
    return jnp.dot(a, b, preferred_element_type=F32, precision=prec)


def _dot_nt(a, b, prec=None):
    return lax.dot_general(a, b, (((1,), (1,)), ((), ())), preferred_element_type=F32, precision=prec)


def _dot_tn(a, b, prec=None):
    return lax.dot_general(a, b, (((0,), (0,)), ((), ())), preferred_element_type=F32, precision=prec)


def _bf(x):
    return x.astype(BF16)


def _layer_norm(x, g, b):
    mu = jnp.mean(x, -1, keepdims=True)
    xc = x - mu
    var = jnp.mean(xc * xc, -1, keepdims=True)
    return xc * lax.rsqrt(var + NORM_EPS) * g + b


def _softplus(x):
    return jnp.maximum(x, 0.0) + jnp.log1p(jnp.exp(-jnp.abs(x)))


def _sigmoid(x):
    return jax.nn.sigmoid(x)


def _silu(x):
    return x * jax.nn.sigmoid(x)


def _iota(shape, axis):
    return lax.broadcasted_iota(jnp.int32, shape, axis)


def _block_mask(rows, cols, rblk, cblk):
    return (_iota((rows, cols), 0) // rblk) == (_iota((rows, cols), 1) // cblk)


def _bd(y, mask_bf, reps):
    return jnp.concatenate([y] * reps, axis=0) * mask_bf


def _seg_sum(x, ones_bf):
    n = x.shape[0]
    outs = []
    for hf in range(x.shape[1] // HALF):
        xh = x[:, hf * HALF:(hf + 1) * HALF]
        hi = _bf(xh)
        lo = _bf(xh - hi.astype(F32))
        s = _dot(jnp.concatenate([hi, lo], axis=0), ones_bf)
        outs.append(s[0:n] + s[n:2 * n])
    return jnp.concatenate(outs, axis=1)


def _chunk_masks(n, width):
    t = _iota((n, width), 0)
    s = _iota((n, width), 1) % n
    return t, s


def _tri_inv(m, t, s, bdmask_bf):
    reps = m.shape[1] // m.shape[0]

    def mm(a, b):
        return _dot(_bf(a), _bd(_bf(b), bdmask_bf, reps))

    eye = (t == s).astype(F32)
    same16 = (t // 16) == (s // 16)
    same32 = (t // 32) == (s // 32)
    m16 = jnp.where(same16, m, 0.0)
    x = eye - m16
    p = mm(m16, m16)
    x = x + mm(x, p)
    p = mm(p, p)
    x = x + mm(x, p)
    p = mm(p, p)
    x = x + mm(x, p)
    o32 = jnp.where(same32 & jnp.logical_not(same16), m, 0.0)
    x = x - mm(x, mm(o32, x))
    o64 = jnp.where(same32, 0.0, m)
    x = x - mm(x, mm(o64, x))
    return x


def _ln_kernel(x_ref, g_ref, b_ref, o_ref, ob_ref):
    y = _layer_norm(x_ref[...], g_ref[...], b_ref[...])
    o_ref[...] = y
    ob_ref[...] = y.astype(BF16)


def _ln_in(x, g, b, tm=256):
    m, d = x.shape
    return pl.pallas_call(
        _ln_kernel,
        grid=(m // tm,),
        in_specs=[pl.BlockSpec((tm, d), lambda i: (i, 0)),
                  pl.BlockSpec((1, d), lambda i: (0, 0)),
                  pl.BlockSpec((1, d), lambda i: (0, 0))],
        out_specs=[pl.BlockSpec((tm, d), lambda i: (i, 0)),
                   pl.BlockSpec((tm, d), lambda i: (i, 0))],
        out_shape=[jax.ShapeDtypeStruct((m, d), F32), jax.ShapeDtypeStruct((m, d), BF16)],
        compiler_params=_cparams(("parallel",)),
        name="ln_in",
    )(x, g.reshape(1, d), b.reshape(1, d))


def _mm_kernel(a_ref, w_ref, o_ref):
    o_ref[...] = _dot(a_ref[...], w_ref[...]).astype(o_ref.dtype)


def _matmul(a, w, tm, tn, out_dtype, name):
    m, k = a.shape
    n = w.shape[1]
    tm = min(tm, m)
    return pl.pallas_call(
        _mm_kernel,
        grid=(m // tm, n // tn),
        in_specs=[pl.BlockSpec((tm, k), lambda i, j: (i, 0)),
                  pl.BlockSpec((k, tn), lambda i, j: (0, j))],
        out_specs=pl.BlockSpec((tm, tn), lambda i, j: (i, j)),
        out_shape=jax.ShapeDtypeStruct((m, n), out_dtype),
        compiler_params=_cparams(("parallel", "parallel")),
        name=name,
    )(a, w)


def _rwkv_kernel(main_ref, small_ref, mu_main_ref, mu_small_ref, w0_ref, w2_ref, a0_ref, a2_ref, g2_ref,
                 kk_ref, ka_ref, rk_ref, lnw_ref, lnb_ref, o_ref, xbuf, state):
    c = pl.program_id(1)
    n = CHUNK
    wm = 3 * A_WIDTH
    reps = HALF // A_DIM

    @pl.when(c == 0)
    def _():
        xbuf[0:HALO, :] = jnp.zeros((HALO, xbuf.shape[1]), F32)
        state[...] = jnp.zeros(state.shape, F32)

    raw_main = main_ref[...]
    raw_small = small_ref[...]
    xbuf[HALO:HALO + n, 0:wm] = raw_main
    xbuf[HALO:HALO + n, wm:] = raw_small
    prev = xbuf[HALO - 1:HALO - 1 + n, :]
    xbuf[0:HALO, :] = xbuf[n:n + HALO, :]
    x_main = raw_main + (prev[:, 0:wm] - raw_main) * mu_main_ref[...]
    x_small = raw_small + (prev[:, wm:] - raw_small) * mu_small_ref[...]

    r = x_main[:, 0:A_WIDTH]
    k = x_main[:, A_WIDTH:2 * A_WIDTH]
    v = x_main[:, 2 * A_WIDTH:3 * A_WIDTH]
    wlog = -_softplus(-(w0_ref[...] + _dot(jnp.tanh(x_small), w2_ref[...], HI))) - 0.5
    logdec = -jnp.exp(wlog)
    a = _sigmoid(a0_ref[...] + _dot(x_small, a2_ref[...], HI))
    g = _dot(_sigmoid(x_small), g2_ref[...], HI)

    bdmask = _block_mask(HALF, HALF, A_DIM, A_DIM)
    bdmask_bf = bdmask.astype(F32).astype(BF16)
    t, s = _chunk_masks(n, HALF)
    incl = t >= s
    strict = t > s

    kkr = k * kk_ref[...]
    kk = kkr * lax.rsqrt(_seg_sum(kkr * kkr, bdmask_bf) + L2_EPS)
    k = k * (1.0 + (a - 1.0) * ka_ref[...])
    bonus = _seg_sum(r * k * rk_ref[...], bdmask_bf) * v

    tri = (_iota((n, n), 0) >= _iota((n, n), 1)).astype(F32)
    cum = _dot(tri, logdec, HI)
    e_neg = jnp.exp(-cum)
    e_rest = jnp.exp(cum[n - 1:n, :] - cum)
    e_last = jnp.exp(cum[n - 1:n, :])
    a_t = -kk * jnp.exp(cum - logdec)
    r_t = r * jnp.exp(cum)
    b = kk * a
    b_t = b * e_neg
    k_t = k * e_neg
    b_c = b * e_rest
    k_c = k * e_rest

    outs = []
    for hf in range(A_WIDTH // HALF):
        sl = slice(hf * HALF, (hf + 1) * HALF)
        s0 = state[hf]
        v_h = v[:, sl]
        bdv = _bd(_bf(v_h), bdmask_bf, reps)
        lhs = _bf(jnp.concatenate([a_t[:, sl], r_t[:, sl]], axis=0))
        rhs = jnp.concatenate([_bd(_bf(b_t[:, sl]), bdmask_bf, reps),
                               _bd(_bf(k_t[:, sl]), bdmask_bf, reps), _bf(s0)], axis=0)
        pr = _dot_nt(lhs, rhs)
        ab = jnp.where(strict, pr[0:n, 0:HALF], 0.0)
        ak = jnp.where(strict, pr[0:n, HALF:2 * HALF], 0.0)
        rb = jnp.where(incl, pr[n:2 * n, 0:HALF], 0.0)
        rk = jnp.where(incl, pr[n:2 * n, HALF:2 * HALF], 0.0)
        rhs_u = pr[0:n, 2 * HALF:] + _dot(_bf(ak), bdv)
        tinv = _tri_inv(-ab, t, s, bdmask_bf)
        u = _dot(_bf(tinv), _bd(_bf(rhs_u), bdmask_bf, reps))
        o = pr[n:2 * n, 2 * HALF:] + _dot(
            _bf(jnp.concatenate([rb, rk], axis=1)),
            jnp.concatenate([_bd(_bf(u), bdmask_bf, reps), bdv], axis=0))
        upd = _dot_tn(_bf(jnp.concatenate([u, v_h], axis=0)),
                      _bf(jnp.concatenate([b_c[:, sl], k_c[:, sl]], axis=0)))
        state[hf] = s0 * e_last[:, sl] + jnp.where(bdmask, upd, 0.0)
        outs.append(o)
    o = jnp.concatenate(outs, axis=1)
    oc = o - _seg_sum(o, bdmask_bf) * (1.0 / A_DIM)
    var = _seg_sum(oc * oc, bdmask_bf) * (1.0 / A_DIM)
    on = oc * lax.rsqrt(var + A_GN_EPS)
    o_ref[...] = ((on * lnw_ref[...] + lnb_ref[...] + bonus) * g).astype(o_ref.dtype)


def _rwkv(p, bsz, seq, mu_main, mu_small, w0, w2p, a0, a2p, g2p, kk, ka, rk, lnw, lnb):
    nc = seq // CHUNK
    n = CHUNK
    row = lambda width, blk: pl.BlockSpec((n, width), lambda b, c: (b * nc + c, blk))
    full = lambda arr: pl.BlockSpec(arr.shape, lambda b, c: (0,) * arr.ndim)
    params = (mu_main, mu_small, w0, w2p, a0, a2p, g2p, kk, ka, rk, lnw, lnb)
    return pl.pallas_call(
        _rwkv_kernel,
        grid=(bsz, nc),
        in_specs=[row(3 * A_WIDTH, P_A_MAIN // (3 * A_WIDTH)), row(256, P_A_SMALL // 256)]
                 + [full(t) for t in params],
        out_specs=pl.BlockSpec((n, A_WIDTH), lambda b, c: (b * nc + c, 0)),
        out_shape=jax.ShapeDtypeStruct((bsz * seq, A_WIDTH), BF16),
        scratch_shapes=[pltpu.VMEM((n + HALO, 3 * A_WIDTH + 256), F32),
                        pltpu.VMEM((A_WIDTH // HALF, HALF, HALF), F32)],
        compiler_params=_cparams(("arbitrary", "arbitrary")),
        name="rwkv7",
    )(p, p, *params)


def _gla_kernel(main_ref, small_ref, gkw_ref, gkb_ref, nw_ref, o_ref, state):
    c = pl.program_id(1)
    n = CHUNK

    @pl.when(c == 0)
    def _():
        state[...] = jnp.zeros(state.shape, F32)

    x = main_ref[...]
    q = x[:, 0:B_KEY_WIDTH] * (B_KEY ** -0.5)
    k = x[:, B_KEY_WIDTH:2 * B_KEY_WIDTH]
    v = x[:, 2 * B_KEY_WIDTH:2 * B_KEY_WIDTH + B_WIDTH]
    g = x[:, 2 * B_KEY_WIDTH + B_WIDTH:]
    gk = -_softplus(-(_dot(small_ref[...], gkw_ref[...], HI) + gkb_ref[...])) / B_GATE_NORMALIZER
    tri = (_iota((n, n), 0) >= _iota((n, n), 1)).astype(F32)
    bcum = _dot(tri, gk, HI)
    qd = q * jnp.exp(bcum)
    kd = k * jnp.exp(-bcum)
    k_c = k * jnp.exp(bcum[n - 1:n, :] - bcum)
    e_last = jnp.exp(bcum[n - 1:n, :])

    t, s = _chunk_masks(n, B_HEADS * n)
    kmask_bf = _block_mask(B_HEADS * n, B_KEY_WIDTH, n, B_KEY).astype(F32).astype(BF16)
    vmask_bf = _block_mask(B_HEADS * n, B_WIDTH, n, B_VAL).astype(F32).astype(BF16)
    smask = _block_mask(B_WIDTH, B_KEY_WIDTH, B_VAL, B_KEY)
    ones128_bf = _block_mask(HALF, HALF, B_VAL, B_VAL).astype(F32).astype(BF16)

    att = jnp.where(t >= s, _dot_nt(_bf(qd), _bd(_bf(kd), kmask_bf, B_HEADS)), 0.0)
    s0 = state[...]
    o = _dot(_bf(att), _bd(_bf(v), vmask_bf, B_HEADS)) + _dot_nt(_bf(qd), _bf(s0))
    state[...] = s0 * e_last + jnp.where(smask, _dot_tn(_bf(v), _bf(k_c)), 0.0)
    o = o * lax.rsqrt(_seg_sum(o * o, ones128_bf) * (1.0 / B_VAL) + NORM_EPS) * nw_ref[...]
    o_ref[...] = (o * _silu(g)).astype(o_ref.dtype)


def _gla(p, bsz, seq, gkw_p, gkb, nw):
    nc = seq // CHUNK
    n = CHUNK
    full = lambda arr: pl.BlockSpec(arr.shape, lambda b, c: (0,) * arr.ndim)
    params = (gkw_p, gkb, nw)
    return pl.pallas_call(
        _gla_kernel,
        grid=(bsz, nc),
        in_specs=[pl.BlockSpec((n, 1536), lambda b, c: (b * nc + c, P_B_MAIN // 1536)),
                  pl.BlockSpec((n, 128), lambda b, c: (b * nc + c, P_B_SMALL // 128))]
                 + [full(t) for t in params],
        out_specs=pl.BlockSpec((n, B_WIDTH), lambda b, c: (b * nc + c, 0)),
        out_shape=jax.ShapeDtypeStruct((bsz * seq, B_WIDTH), BF16),
        scratch_shapes=[pltpu.VMEM((B_WIDTH, B_KEY_WIDTH), F32)],
        compiler_params=_cparams(("arbitrary", "arbitrary")),
        name="gla",
    )(p, p, *params)


def _gdn_kernel(qkv_ref, z_ref, small_ref, cw_ref, alog_ref, dtb_ref, nw_ref, o_ref, xbuf, state):
    c = pl.program_id(1)
    n = CHUNK
    hn = C_HEADS * n

    @pl.when(c == 0)
    def _():
        xbuf[0:HALO, :] = jnp.zeros((HALO, xbuf.shape[1]), F32)
        state[...] = jnp.zeros(state.shape, F32)

    raw = qkv_ref[...]
    xbuf[HALO:HALO + n, :] = raw
    y = raw * cw_ref[C_CONV - 1:C_CONV, :]
    for j in range(1, C_CONV):
        y = y + xbuf[HALO - j:HALO - j + n, :] * cw_ref[C_CONV - 1 - j:C_CONV - j, :]
    xbuf[0:HALO, :] = xbuf[n:n + HALO, :]
    qkv = _silu(y)
    q = qkv[:, 0:C_WIDTH]
    k = qkv[:, C_WIDTH:2 * C_WIDTH]
    v = qkv[:, 2 * C_WIDTH:]

    smask = _block_mask(HALF, HALF, C_DIM, C_DIM)
    ones128_bf = smask.astype(F32).astype(BF16)
    q = q * (lax.rsqrt(_seg_sum(q * q, ones128_bf) + L2_EPS) * (C_DIM ** -0.5))
    k = k * lax.rsqrt(_seg_sum(k * k, ones128_bf) + L2_EPS)

    small = small_ref[...]
    g_all = -jnp.exp(alog_ref[...]) * _softplus(small + dtb_ref[...])
    beta_all = _sigmoid(small)
    tri = (_iota((n, n), 0) >= _iota((n, n), 1)).astype(F32)
    gam_s = _dot(tri, g_all, HI)
    gam_w = _dot(gam_s, (_iota((128, C_WIDTH), 0) == _iota((128, C_WIDTH), 1) // C_DIM).astype(F32), HI)
    gam_n = _dot(gam_s, (_iota((128, hn), 0) == _iota((128, hn), 1) // n).astype(F32), HI)
    beta = _dot(beta_all, (_iota((128, C_WIDTH), 0) == C_HEADS + _iota((128, C_WIDTH), 1) // C_DIM).astype(F32),
                HI)

    t, s = _chunk_masks(n, hn)
    incl = t >= s
    strict = t > s
    grow = jnp.sum(jnp.where(t == s, gam_n, 0.0), axis=0, keepdims=True)
    dec = jnp.exp(jnp.where(incl, gam_n - grow, 0.0))
    bdmask_bf = _block_mask(hn, hn, n, n).astype(F32).astype(BF16)
    dmask_bf = _block_mask(hn, C_WIDTH, n, C_DIM).astype(F32).astype(BF16)

    kb = k * beta
    pr = _dot_nt(_bf(jnp.concatenate([kb, q], axis=0)), _bd(_bf(k), dmask_bf, C_HEADS))
    m = jnp.where(strict, pr[0:n] * dec, 0.0)
    qk = jnp.where(incl, pr[n:2 * n] * dec, 0.0)
    tinv = _tri_inv(m, t, s, bdmask_bf)
    eg = jnp.exp(gam_w)
    uw = _dot(_bf(tinv), jnp.concatenate([_bd(_bf(v * beta), dmask_bf, C_HEADS),
                                          _bd(_bf(kb * eg), dmask_bf, C_HEADS)], axis=1))
    u = uw[:, 0:C_WIDTH]
    w = uw[:, C_WIDTH:]
    glast = gam_w[n - 1:n, :]
    e_last = jnp.exp(glast)
    kg = k * jnp.exp(glast - gam_w)
    qg = q * eg
    v_new, o_inter = [], []
    for hf in range(C_WIDTH // HALF):
        sl = slice(hf * HALF, (hf + 1) * HALF)
        s0 = state[hf]
        ws = _dot(_bf(jnp.concatenate([w[:, sl], qg[:, sl]], axis=0)), _bf(s0))
        vn = u[:, sl] - ws[0:n]
        state[hf] = s0 * e_last[:, sl] + jnp.where(smask, _dot_tn(_bf(kg[:, sl]), _bf(vn)), 0.0)
        v_new.append(vn)
        o_inter.append(ws[n:2 * n])
    v_new = jnp.concatenate(v_new, axis=1)
    o = jnp.concatenate(o_inter, axis=1) + _dot(_bf(qk), _bd(_bf(v_new), dmask_bf, C_HEADS))
    o = o * lax.rsqrt(_seg_sum(o * o, ones128_bf) * (1.0 / C_DIM) + NORM_EPS) * nw_ref[...]
    o_ref[...] = (o * _silu(z_ref[...])).astype(o_ref.dtype)


def _gdn(p, bsz, seq, cw, alog_p, dtb_p, nw):
    nc = seq // CHUNK
    n = CHUNK
    full = lambda arr: pl.BlockSpec(arr.shape, lambda b, c: (0,) * arr.ndim)
    params = (cw, alog_p, dtb_p, nw)
    return pl.pallas_call(
        _gdn_kernel,
        grid=(bsz, nc),
        in_specs=[pl.BlockSpec((n, 1536), lambda b, c: (b * nc + c, P_C_QKV // 1536)),
                  pl.BlockSpec((n, 512), lambda b, c: (b * nc + c, P_C_Z // 512)),
                  pl.BlockSpec((n, 128), lambda b, c: (b * nc + c, P_C_SMALL // 128))]
                 + [full(t) for t in params],
        out_specs=pl.BlockSpec((n, C_WIDTH), lambda b, c: (b * nc + c, 0)),
        out_shape=jax.ShapeDtypeStruct((bsz * seq, C_WIDTH), BF16),
        scratch_shapes=[pltpu.VMEM((n + HALO, 3 * C_WIDTH), F32),
                        pltpu.VMEM((C_WIDTH // HALF, HALF, HALF), F32)],
        compiler_params=_cparams(("arbitrary", "arbitrary")),
        name="gdn",
    )(p, p, p, *params)


def _merge_kernel(h_ref, ya_ref, yb_ref, yc_ref, wg_ref, wb_ref, o_ref):
    h = h_ref[...]
    acc = None
    for i, y_ref in enumerate((ya_ref, yb_ref, yc_ref)):
        t = _sigmoid(_dot(h, wg_ref[i])) * _dot(y_ref[...], wb_ref[i])
        acc = t if acc is None else acc + t
    o_ref[...] = acc.astype(o_ref.dtype)


def _merge(hb, ya, yb, yc, wg, wb, tm=1024, tn=256):
    m, d = hb.shape
    bw = ya.shape[1]
    tm = min(tm, m)
    yspec = pl.BlockSpec((tm, bw), lambda i, j: (i, 0))
    return pl.pallas_call(
        _merge_kernel,
        grid=(m // tm, d // tn),
        in_specs=[pl.BlockSpec((tm, d), lambda i, j: (i, 0)), yspec, yspec, yspec,
                  pl.BlockSpec((3, d, tn), lambda i, j: (0, 0, j)),
                  pl.BlockSpec((3, bw, tn), lambda i, j: (0, 0, j))],
        out_specs=pl.BlockSpec((tm, tn), lambda i, j: (i, j)),
        out_shape=jax.ShapeDtypeStruct((m, d), BF16),
        compiler_params=_cparams(("parallel", "parallel")),
        name="merge",
    )(hb, ya, yb, yc, wg, wb)


def _proj_ln_kernel(a_ref, w_ref, h_ref, g_ref, b_ref, o_ref, ob_ref, acc_ref):
    kstep = pl.program_id(1)

    @pl.when(kstep == 0)
    def _():
        acc_ref[...] = jnp.zeros(acc_ref.shape, F32)

    acc_ref[...] += _dot(a_ref[...], w_ref[...])

    @pl.when(kstep == pl.num_programs(1) - 1)
    def _():
        y = _layer_norm(DEEPNORM_ALPHA * h_ref[...] + acc_ref[...], g_ref[...], b_ref[...])
        o_ref[...] = y
        ob_ref[...] = y.astype(BF16)


def _proj_ln(a, w, h, g, b, tm, tk, name):
    m, k = a.shape
    d = w.shape[1]
    tm = min(tm, m)
    return pl.pallas_call(
        _proj_ln_kernel,
        grid=(m // tm, k // tk),
        in_specs=[pl.BlockSpec((tm, tk), lambda i, j: (i, j)),
                  pl.BlockSpec((tk, d), lambda i, j: (j, 0)),
                  pl.BlockSpec((tm, d), lambda i, j: (i, 0)),
                  pl.BlockSpec((1, d), lambda i, j: (0, 0)),
                  pl.BlockSpec((1, d), lambda i, j: (0, 0))],
        out_specs=[pl.BlockSpec((tm, d), lambda i, j: (i, 0)),
                   pl.BlockSpec((tm, d), lambda i, j: (i, 0))],
        out_shape=[jax.ShapeDtypeStruct((m, d), F32), jax.ShapeDtypeStruct((m, d), BF16)],
        scratch_shapes=[pltpu.VMEM((tm, d), F32)],
        compiler_params=_cparams(("parallel", "arbitrary")),
        name=name,
    )(a, w, h, g.reshape(1, d), b.reshape(1, d))


def _ffn_up_kernel(a_ref, wg_ref, wu_ref, cwg_ref, cwu_ref, bg_ref, bu_ref, o_ref, gbuf, ubuf, gcar, ucar,
                   *, blocks_per_seq):
    i = pl.program_id(0)
    j = pl.program_id(1)
    tm = a_ref.shape[0]
    a = a_ref[...]
    first = (i % blocks_per_seq) == 0

    def conv(w_ref, cw_ref, b_ref, buf, car):
        u = _dot(a, w_ref[...])
        buf[HALO:HALO + tm, :] = u
        buf[0:HALO, :] = jnp.where(first, 0.0, car[j])
        car[j] = u[tm - HALO:tm, :]
        return (u * cw_ref[2:3, :] + buf[HALO - 1:HALO - 1 + tm, :] * cw_ref[1:2, :]
                + buf[HALO - 2:HALO - 2 + tm, :] * cw_ref[0:1, :] + b_ref[...])

    gate = conv(wg_ref, cwg_ref, bg_ref, gbuf, gcar)
    up = conv(wu_ref, cwu_ref, bu_ref, ubuf, ucar)
    o_ref[...] = (_silu(gate) * up).astype(o_ref.dtype)


def _ffn_up(hb, w_up, conv_w, conv_b, seq, tm=1024, tn=512):
    m, d = hb.shape
    nf = w_up.shape[1] // 2
    nj = nf // tn
    tm = min(tm, seq)
    kern = functools.partial(_ffn_up_kernel, blocks_per_seq=seq // tm)
    return pl.pallas_call(
        kern,
        grid=(m // tm, nj),
        in_specs=[pl.BlockSpec((tm, d), lambda i, j: (i, 0)),
                  pl.BlockSpec((d, tn), lambda i, j: (0, j)),
                  pl.BlockSpec((d, tn), lambda i, j: (0, j + nj)),
                  pl.BlockSpec((FFN_CONV, tn), lambda i, j: (0, j)),
                  pl.BlockSpec((FFN_CONV, tn), lambda i, j: (0, j + nj)),
                  pl.BlockSpec((1, tn), lambda i, j: (0, j)),
                  pl.BlockSpec((1, tn), lambda i, j: (0, j + nj))],
        out_specs=pl.BlockSpec((tm, tn), lambda i, j: (i, j)),
        out_shape=jax.ShapeDtypeStruct((m, nf), BF16),
        scratch_shapes=[pltpu.VMEM((tm + HALO, tn), F32), pltpu.VMEM((tm + HALO, tn), F32),
                        pltpu.VMEM((nj, HALO, tn), F32), pltpu.VMEM((nj, HALO, tn), F32)],
        compiler_params=_cparams(("arbitrary", "arbitrary")),
        name="ffn_up",
    )(hb, w_up, w_up, conv_w, conv_w, conv_b, conv_b)


def _pad_cols(t, width):
    return jnp.pad(t, ((0, 0), (0, width - t.shape[1])))


def _pad_rows(t, before, total):
    return jnp.pad(t, ((before, total - before - t.shape[0]), (0, 0)))


def _relayout_w_in(w):
    a_in = 3 * A_WIDTH + A_DECAY_LORA + A_ICLR_LORA + A_GATE_LORA
    b_in = 2 * B_KEY_WIDTH + B_WIDTH + B_GATE_LORA + B_WIDTH
    wa, wb, wc = w[:, :a_in], w[:, a_in:a_in + b_in], w[:, a_in + b_in:]
    b_qkv = wb[:, :2 * B_KEY_WIDTH + B_WIDTH]
    b_gk = wb[:, 2 * B_KEY_WIDTH + B_WIDTH:2 * B_KEY_WIDTH + B_WIDTH + B_GATE_LORA]
    b_g = wb[:, 2 * B_KEY_WIDTH + B_WIDTH + B_GATE_LORA:]
    c_qkv = wc[:, :3 * C_WIDTH]
    c_ab = wc[:, 3 * C_WIDTH:3 * C_WIDTH + 2 * C_HEADS]
    c_z = wc[:, 3 * C_WIDTH + 2 * C_HEADS:]
    return jnp.concatenate([
        wa[:, :3 * A_WIDTH], b_qkv, b_g, c_qkv, c_z,
        _pad_cols(wa[:, 3 * A_WIDTH:], 256), _pad_cols(b_gk, 128), _pad_cols(c_ab, 128)], axis=1)


def kernel(x, ln_in_g, ln_in_b, w_in, mu_a, a_w0, a_w2, a_a0, a_a2, a_g2, a_kk, a_ka, a_rk, a_lnx_w, a_lnx_b,
           b_gk_w2, b_gk_b, b_norm_w, c_conv_w, c_a_log, c_dt_bias, c_norm_w, w_gate, w_branch, w_out,
           ln1_g, ln1_b, w_up, ffn_conv_w, ffn_conv_b, w_down, ln2_g, ln2_b):
    bsz, seq, d = x.shape
    m = bsz * seq
    assert seq % CHUNK == 0
    h, hb = _ln_in(x.reshape(m, d), ln_in_g, ln_in_b)
    r1 = lambda t: t.reshape(1, -1)
    for l in range(DEPTH):
        p = _matmul(hb, _relayout_w_in(w_in[l]).astype(BF16), 1024, 512, F32, "in_proj")
        lo = A_DECAY_LORA
        ya = _rwkv(
            p, bsz, seq,
            r1(mu_a[l, :3 * A_WIDTH]), _pad_cols(r1(mu_a[l, 3 * A_WIDTH:]), 256),
            r1(a_w0[l]), _pad_rows(a_w2[l], 0, 256),
            r1(a_a0[l]), _pad_rows(a_a2[l], lo, 256),
            _pad_rows(a_g2[l], lo + A_ICLR_LORA, 256),
            r1(a_kk[l]), r1(a_ka[l]), r1(a_rk[l]), r1(a_lnx_w[l]), r1(a_lnx_b[l]))
        yb = _gla(p, bsz, seq, _pad_rows(b_gk_w2[l], 0, 128), r1(b_gk_b[l]),
                  jnp.tile(r1(b_norm_w[l]), (1, B_HEADS)))
        yc = _gdn(p, bsz, seq, c_conv_w[l], _pad_cols(r1(c_a_log[l]), 128), _pad_cols(r1(c_dt_bias[l]), 128),
                  jnp.tile(r1(c_norm_w[l]), (1, C_HEADS)))
        merged = _merge(hb, ya, yb, yc, w_gate[l].astype(BF16), w_branch[l].astype(BF16))
        h, hb = _proj_ln(merged, w_out[l].astype(BF16), h, ln1_g[l], ln1_b[l], 256, d, "out_proj")
        act = _ffn_up(hb, w_up[l].astype(BF16), ffn_conv_w[l], r1(ffn_conv_b[l]), seq)
        h, hb = _proj_ln(act, w_down[l].astype(BF16), h, ln2_g[l], ln2_b[l], 512, 512, "ffn_down")
    return h.reshape(bsz, seq, d)
```

```python
import functools

import jax
import jax.numpy as jnp
from jax import lax
from jax.experimental import pallas as pl
from jax.experimental.pallas import tpu as pltpu

F32 = jnp.float32
BF16 = jnp.bfloat16

DEPTH = 2
A_HEADS, A_DIM, A_WIDTH = 8, 64, 512
A_DECAY_LORA, A_ICLR_LORA, A_GATE_LORA = 32, 32, 96
A_GN_EPS = 64e-5
B_HEADS, B_KEY, B_VAL = 4, 64, 128
B_KEY_WIDTH, B_WIDTH = 256, 512
B_GATE_LORA = 16
B_GATE_NORMALIZER = 16.0
C_HEADS, C_DIM, C_WIDTH = 4, 128, 512
C_CONV = 4
D_FF = 5632
FFN_CONV = 3
NORM_EPS = 1e-5
L2_EPS = 1e-6
DEEPNORM_ALPHA = (2 * DEPTH) ** 0.25

CHUNK = 64
HALO = 8
HALF = 256
SMALL = 128

P_A_MAIN = 0
P_B_MAIN = 1536
P_C_QKV = 3072
P_C_Z = 4608
P_A_SMALL = 5120
P_B_SMALL = 5376
P_C_SMALL = 5504
P_WIDTH = 5632

VMEM_LIMIT = 56 * 1024 * 1024


def _cparams(sem):
    return pltpu.CompilerParams(dimension_semantics=sem, vmem_limit_bytes=VMEM_LIMIT)


def _dot(a, b):
    return jnp.dot(a, b, preferred_element_type=F32)


def _dot_nt(a, b):
    return lax.dot_general(a, b, (((1,), (1,)), ((), ())), preferred_element_type=F32)


def _dot_tn(a, b):
    return lax.dot_general(a, b, (((0,), (0,)), ((), ())), preferred_element_type=F32)


def _bf(x):
    return x.astype(BF16)


def _layer_norm(x, g, b):
    mu = jnp.mean(x, -1, keepdims=True)
    xc = x - mu
    var = jnp.mean(xc * xc, -1, keepdims=True)
    return xc * lax.rsqrt(var + NORM_EPS) * g + b


def _softplus(x):
    return jnp.maximum(x, 0.0) + jnp.log1p(jnp.exp(-jnp.abs(x)))


def _sigmoid(x):
    return jax.nn.sigmoid(x)


def _silu(x):
    return x * jax.nn.sigmoid(x)


def _iota(shape, axis):
    return lax.broadcasted_iota(jnp.int32, shape, axis)


def _mask_bf(mask):
    return mask.astype(F32).astype(BF16)


def _block_mask(rows, cols, rblk, cblk):
    return (_iota((rows, cols), 0) // rblk) == (_iota((rows, cols), 1) // cblk)


def _bd(y, mask_bf, reps):
    return jnp.concatenate([y] * reps, axis=0) * mask_bf


def _pieces(x, count):
    out = []
    for _ in range(count - 1):
        p = _bf(x)
        out.append(p)
        x = x - p.astype(F32)
    out.append(_bf(x))
    return out


def _exact_rows(x, w_bf, count=3):
    n = x.shape[0]
    s = _dot(jnp.concatenate(_pieces(x, count), axis=0), w_bf)
    return sum(s[i * n:(i + 1) * n] for i in range(count))


def _exact_cols(w_bf, x, count=3):
    c = x.shape[1]
    s = _dot(w_bf, jnp.concatenate(_pieces(x, count), axis=1))
    return sum(s[:, i * c:(i + 1) * c] for i in range(count))


def _seg_sum(x, ones_bf):
    return jnp.concatenate([_exact_rows(x[:, i:i + HALF], ones_bf, 2) for i in range(0, x.shape[1], HALF)],
                           axis=1)


def _chunk_masks(n, width):
    return _iota((n, width), 0), _iota((n, width), 1) % n


def _tri_inv(ms, t, s, bdmask_bf):
    reps = ms[0].shape[1] // ms[0].shape[0]

    def mm(a_list, b_list):
        return [_dot(_bf(a), _bd(_bf(b), bdmask_bf, reps)) for a, b in zip(a_list, b_list)]

    def add(a_list, b_list):
        return [a + b for a, b in zip(a_list, b_list)]

    def sub(a_list, b_list):
        return [a - b for a, b in zip(a_list, b_list)]

    eye = (t == s).astype(F32)
    same16 = (t // 16) == (s // 16)
    same32 = (t // 32) == (s // 32)
    m16 = [jnp.where(same16, m, 0.0) for m in ms]
    x = [eye - m for m in m16]
    p = mm(m16, m16)
    x = add(x, mm(x, p))
    p = mm(p, p)
    x = add(x, mm(x, p))
    p = mm(p, p)
    x = add(x, mm(x, p))
    o32 = [jnp.where(same32 & jnp.logical_not(same16), m, 0.0) for m in ms]
    x = sub(x, mm(x, mm(o32, x)))
    o64 = [jnp.where(same32, 0.0, m) for m in ms]
    x = sub(x, mm(x, mm(o64, x)))
    return x


def _ln_kernel(x_ref, g_ref, b_ref, o_ref, ob_ref):
    y = _layer_norm(x_ref[...], g_ref[...], b_ref[...])
    o_ref[...] = y
    ob_ref[...] = y.astype(BF16)


def _ln_in(x, g, b, tm=256):
    m, d = x.shape
    return pl.pallas_call(
        _ln_kernel,
        grid=(m // tm,),
        in_specs=[pl.BlockSpec((tm, d), lambda i: (i, 0)),
                  pl.BlockSpec((1, d), lambda i: (0, 0)),
                  pl.BlockSpec((1, d), lambda i: (0, 0))],
        out_specs=[pl.BlockSpec((tm, d), lambda i: (i, 0)),
                   pl.BlockSpec((tm, d), lambda i: (i, 0))],
        out_shape=[jax.ShapeDtypeStruct((m, d), F32), jax.ShapeDtypeStruct((m, d), BF16)],
        compiler_params=_cparams(("parallel",)),
        name="ln_in",
    )(x, g.reshape(1, d), b.reshape(1, d))


def _mm_kernel(a_ref, w_ref, o_ref):
    o_ref[...] = _dot(a_ref[...], w_ref[...]).astype(o_ref.dtype)


def _matmul(a, w, tm, tn, out_dtype, name):
    m, k = a.shape
    n = w.shape[1]
    tm = min(tm, m)
    return pl.pallas_call(
        _mm_kernel,
        grid=(m // tm, n // tn),
        in_specs=[pl.BlockSpec((tm, k), lambda i, j: (i, 0)),
                  pl.BlockSpec((k, tn), lambda i, j: (0, j))],
        out_specs=pl.BlockSpec((tm, tn), lambda i, j: (i, j)),
        out_shape=jax.ShapeDtypeStruct((m, n), out_dtype),
        compiler_params=_cparams(("parallel", "parallel")),
        name=name,
    )(a, w)


def _mixer_call(kern, p, blocks, params, scratch, name):
    bsz, seq, _ = p.shape
    n = CHUNK
    full = lambda arr: pl.BlockSpec(arr.shape, lambda c: (0,) * arr.ndim)
    in_specs = [pl.BlockSpec((bsz, n, width), functools.partial(lambda c, blk: (0, c, blk), blk=col // width))
                for col, width in blocks]
    return pl.pallas_call(
        kern,
        grid=(seq // n,),
        in_specs=in_specs + [full(t) for t in params],
        out_specs=pl.BlockSpec((bsz, n, 512), lambda c: (0, c, 0)),
        out_shape=jax.ShapeDtypeStruct((bsz, seq, 512), BF16),
        scratch_shapes=scratch,
        compiler_params=_cparams(("arbitrary",)),
        name=name,
    )(*([p] * len(blocks)), *params)


def _rwkv_kernel(main_ref, small_ref, mu_main_ref, mu_small_ref, w0_ref, w2_ref, a0_ref, a2_ref, g2_ref,
                 kk_ref, ka_ref, rk_ref, lnw_ref, lnb_ref, o_ref, xbuf, state):
    c = pl.program_id(0)
    nb = main_ref.shape[0]
    n = CHUNK
    wm = 3 * A_WIDTH
    reps = HALF // A_DIM
    halves = A_WIDTH // HALF

    @pl.when(c == 0)
    def _():
        xbuf[:, 0:HALO, :] = jnp.zeros((nb, HALO, xbuf.shape[2]), F32)
        state[...] = jnp.zeros(state.shape, F32)

    bdmask = _block_mask(HALF, HALF, A_DIM, A_DIM)
    bdmask_bf = _mask_bf(bdmask)
    t, s = _chunk_masks(n, HALF)
    incl = t >= s
    strict = t > s
    tri_bf = _mask_bf(_iota((n, n), 0) >= _iota((n, n), 1))

    seqs = []
    for i in range(nb):
        raw_main = main_ref[i]
        raw_small = small_ref[i]
        xbuf[i, HALO:HALO + n, 0:wm] = raw_main
        xbuf[i, HALO:HALO + n, wm:] = raw_small
        prev = xbuf[i, HALO - 1:HALO - 1 + n, :]
        xbuf[i, 0:HALO, :] = xbuf[i, n:n + HALO, :]
        x_main = raw_main + (prev[:, 0:wm] - raw_main) * mu_main_ref[...]
        x_small = raw_small + (prev[:, wm:] - raw_small) * mu_small_ref[...]
        r = x_main[:, 0:A_WIDTH]
        k = x_main[:, A_WIDTH:2 * A_WIDTH]
        v = x_main[:, 2 * A_WIDTH:3 * A_WIDTH]
        wlog = -_softplus(-(w0_ref[...] + _dot(_bf(jnp.tanh(x_small)), w2_ref[...]))) - 0.5
        logdec = -jnp.exp(wlog)
        a = _sigmoid(a0_ref[...] + _dot(_bf(x_small), a2_ref[...]))
        g = _dot(_bf(_sigmoid(x_small)), g2_ref[...])
        kkr = k * kk_ref[...]
        kk = kkr * lax.rsqrt(_seg_sum(kkr * kkr, bdmask_bf) + L2_EPS)
        k = k * (1.0 + (a - 1.0) * ka_ref[...])
        bonus = _seg_sum(r * k * rk_ref[...], bdmask_bf) * v
        cum = _exact_cols(tri_bf, logdec)
        e_neg = jnp.exp(-cum)
        e_rest = jnp.exp(cum[n - 1:n, :] - cum)
        b = kk * a
        seqs.append(dict(
            v=v, g=g, bonus=bonus,
            a_t=-kk * jnp.exp(cum - logdec),
            r_t=r * jnp.exp(cum),
            b_t=b * e_neg, k_t=k * e_neg,
            b_c=b * e_rest, k_c=k * e_rest,
            e_last=jnp.exp(cum[n - 1:n, :])))

    chains = [(i, hf) for i in range(nb) for hf in range(halves)]
    col = lambda name, i, hf: seqs[i][name][:, hf * HALF:(hf + 1) * HALF]
    s_old = [state[i, hf] for i, hf in chains]
    v_h = [col("v", i, hf) for i, hf in chains]
    bdv = [_bd(_bf(x), bdmask_bf, reps) for x in v_h]
    pr = [_dot_nt(_bf(jnp.concatenate([col("a_t", i, hf), col("r_t", i, hf)], axis=0)),
                  jnp.concatenate([_bd(_bf(col("b_t", i, hf)), bdmask_bf, reps),
                                   _bd(_bf(col("k_t", i, hf)), bdmask_bf, reps), _bf(s0)], axis=0))
          for (i, hf), s0 in zip(chains, s_old)]
    ab = [jnp.where(strict, x[0:n, 0:HALF], 0.0) for x in pr]
    rhs_u = [x[0:n, 2 * HALF:] + _dot(_bf(jnp.where(strict, x[0:n, HALF:2 * HALF], 0.0)), y)
             for x, y in zip(pr, bdv)]
    tinv = _tri_inv([-x for x in ab], t, s, bdmask_bf)
    u = [_dot(_bf(x), _bd(_bf(y), bdmask_bf, reps)) for x, y in zip(tinv, rhs_u)]
    o = [x[n:2 * n, 2 * HALF:] + _dot(
            _bf(jnp.concatenate([jnp.where(incl, x[n:2 * n, 0:HALF], 0.0),
                                 jnp.where(incl, x[n:2 * n, HALF:2 * HALF], 0.0)], axis=1)),
            jnp.concatenate([_bd(_bf(y), bdmask_bf, reps), z], axis=0))
         for x, y, z in zip(pr, u, bdv)]
    upd = [_dot_tn(_bf(jnp.concatenate([y, z], axis=0)),
                   _bf(jnp.concatenate([col("b_c", i, hf), col("k_c", i, hf)], axis=0)))
           for (i, hf), y, z in zip(chains, u, v_h)]
    for (i, hf), s0, x in zip(chains, s_old, upd):
        state[i, hf] = s0 * col("e_last", i, hf) + jnp.where(bdmask, x, 0.0)
    for i in range(nb):
        oi = jnp.concatenate(o[i * halves:(i + 1) * halves], axis=1)
        oc = oi - _seg_sum(oi, bdmask_bf) * (1.0 / A_DIM)
        var = _seg_sum(oc * oc, bdmask_bf) * (1.0 / A_DIM)
        on = oc * lax.rsqrt(var + A_GN_EPS)
        o_ref[i] = ((on * lnw_ref[...] + lnb_ref[...] + seqs[i]["bonus"]) * seqs[i]["g"]).astype(o_ref.dtype)


def _rwkv(p, params):
    bsz = p.shape[0]
    return _mixer_call(
        _rwkv_kernel, p, [(P_A_MAIN, 3 * A_WIDTH), (P_A_SMALL, 256)], params,
        [pltpu.VMEM((bsz, CHUNK + HALO, 3 * A_WIDTH + 256), F32),
         pltpu.VMEM((bsz, A_WIDTH // HALF, HALF, HALF), F32)], "rwkv7")


def _gla_kernel(main_ref, small_ref, gkw_ref, gkb_ref, nw_ref, o_ref, state):
    c = pl.program_id(0)
    nb = main_ref.shape[0]
    n = CHUNK

    @pl.when(c == 0)
    def _():
        state[...] = jnp.zeros(state.shape, F32)

    t, s = _chunk_masks(n, B_HEADS * n)
    incl = t >= s
    tri_bf = _mask_bf(_iota((n, n), 0) >= _iota((n, n), 1))
    kmask_bf = _mask_bf(_block_mask(B_HEADS * n, B_KEY_WIDTH, n, B_KEY))
    vmask_bf = _mask_bf(_block_mask(B_HEADS * n, B_WIDTH, n, B_VAL))
    smask = _block_mask(B_WIDTH, B_KEY_WIDTH, B_VAL, B_KEY)
    ones_bf = _mask_bf(_block_mask(HALF, HALF, B_VAL, B_VAL))

    qd, kd, k_c, e_last, v, g = [], [], [], [], [], []
    for i in range(nb):
        x = main_ref[i]
        q = x[:, 0:B_KEY_WIDTH] * (B_KEY ** -0.5)
        k = x[:, B_KEY_WIDTH:2 * B_KEY_WIDTH]
        v.append(x[:, 2 * B_KEY_WIDTH:2 * B_KEY_WIDTH + B_WIDTH])
        g.append(x[:, 2 * B_KEY_WIDTH + B_WIDTH:])
        gk = -_softplus(-(_dot(_bf(small_ref[i]), gkw_ref[...]) + gkb_ref[...])) / B_GATE_NORMALIZER
        bcum = _exact_cols(tri_bf, gk)
        qd.append(q * jnp.exp(bcum))
        kd.append(k * jnp.exp(-bcum))
        k_c.append(k * jnp.exp(bcum[n - 1:n, :] - bcum))
        e_last.append(jnp.exp(bcum[n - 1:n, :]))
    s_old = [state[i] for i in range(nb)]
    att = [jnp.where(incl, _dot_nt(_bf(x), _bd(_bf(y), kmask_bf, B_HEADS)), 0.0) for x, y in zip(qd, kd)]
    o = [_dot(_bf(x), _bd(_bf(y), vmask_bf, B_HEADS)) + _dot_nt(_bf(z), _bf(s0))
         for x, y, z, s0 in zip(att, v, qd, s_old)]
    upd = [_dot_tn(_bf(x), _bf(y)) for x, y in zip(v, k_c)]
    for i in range(nb):
        state[i] = s_old[i] * e_last[i] + jnp.where(smask, upd[i], 0.0)
        oi = o[i] * lax.rsqrt(_seg_sum(o[i] * o[i], ones_bf) * (1.0 / B_VAL) + NORM_EPS) * nw_ref[...]
        o_ref[i] = (oi * _silu(g[i])).astype(o_ref.dtype)


def _gla(p, params):
    bsz = p.shape[0]
    return _mixer_call(_gla_kernel, p, [(P_B_MAIN, 1536), (P_B_SMALL, SMALL)], params,
                       [pltpu.VMEM((bsz, B_WIDTH, B_KEY_WIDTH), F32)], "gla")


def _gdn_kernel(qkv_ref, z_ref, small_ref, cw_ref, alog_ref, dtb_ref, nw_ref, o_ref, xbuf, state):
    c = pl.program_id(0)
    nb = qkv_ref.shape[0]
    n = CHUNK
    hn = C_HEADS * n
    halves = C_WIDTH // HALF

    @pl.when(c == 0)
    def _():
        xbuf[:, 0:HALO, :] = jnp.zeros((nb, HALO, xbuf.shape[2]), F32)
        state[...] = jnp.zeros(state.shape, F32)

    smask = _block_mask(HALF, HALF, C_DIM, C_DIM)
    ones_bf = _mask_bf(smask)
    t, s = _chunk_masks(n, hn)
    incl = t >= s
    strict = t > s
    tri_bf = _mask_bf(_iota((n, n), 0) >= _iota((n, n), 1))
    bdmask_bf = _mask_bf(_block_mask(hn, hn, n, n))
    dmask_bf = _mask_bf(_block_mask(hn, C_WIDTH, n, C_DIM))
    lane = _iota((SMALL, C_WIDTH + hn), 1)
    head_of = jnp.where(lane < C_WIDTH, lane // C_DIM, (lane - C_WIDTH) // n)
    spread_g = _mask_bf(_iota((SMALL, C_WIDTH + hn), 0) == head_of)
    spread_b = _mask_bf(_iota((SMALL, C_WIDTH), 0) == C_HEADS + _iota((SMALL, C_WIDTH), 1) // C_DIM)

    seqs = []
    for i in range(nb):
        raw = qkv_ref[i]
        xbuf[i, HALO:HALO + n, :] = raw
        y = raw * cw_ref[C_CONV - 1:C_CONV, :]
        for j in range(1, C_CONV):
            y = y + xbuf[i, HALO - j:HALO - j + n, :] * cw_ref[C_CONV - 1 - j:C_CONV - j, :]
        xbuf[i, 0:HALO, :] = xbuf[i, n:n + HALO, :]
        qkv = _silu(y)
        q = qkv[:, 0:C_WIDTH]
        k = qkv[:, C_WIDTH:2 * C_WIDTH]
        v = qkv[:, 2 * C_WIDTH:]
        q = q * (lax.rsqrt(_seg_sum(q * q, ones_bf) + L2_EPS) * (C_DIM ** -0.5))
        k = k * lax.rsqrt(_seg_sum(k * k, ones_bf) + L2_EPS)
        small = small_ref[i]
        g_all = -jnp.exp(alog_ref[...]) * _softplus(small + dtb_ref[...])
        gam = _exact_rows(_exact_cols(tri_bf, g_all), spread_g)
        gam_w = gam[:, 0:C_WIDTH]
        gam_n = gam[:, C_WIDTH:]
        beta = _exact_rows(_sigmoid(small), spread_b)
        grow = jnp.sum(jnp.where(t == s, gam_n, 0.0), axis=0, keepdims=True)
        dec = jnp.exp(jnp.where(incl, gam_n - grow, 0.0))
        eg = jnp.exp(gam_w)
        glast = gam_w[n - 1:n, :]
        kb = k * beta
        seqs.append(dict(q=q, k=k, kb=kb, dec=dec, vb=v * beta, kbe=kb * eg, qg=q * eg,
                         kg=k * jnp.exp(glast - gam_w), e_last=jnp.exp(glast)))

    pr = [_dot_nt(_bf(jnp.concatenate([d["kb"], d["q"]], axis=0)), _bd(_bf(d["k"]), dmask_bf, C_HEADS))
          for d in seqs]
    m = [jnp.where(strict, x[0:n] * d["dec"], 0.0) for x, d in zip(pr, seqs)]
    qk = [jnp.where(incl, x[n:2 * n] * d["dec"], 0.0) for x, d in zip(pr, seqs)]
    tinv = _tri_inv(m, t, s, bdmask_bf)
    uw = [_dot(_bf(x), jnp.concatenate([_bd(_bf(d["vb"]), dmask_bf, C_HEADS),
                                        _bd(_bf(d["kbe"]), dmask_bf, C_HEADS)], axis=1))
          for x, d in zip(tinv, seqs)]
    chains = [(i, hf) for i in range(nb) for hf in range(halves)]
    s_old = [state[i, hf] for i, hf in chains]
    ws = [_dot(_bf(jnp.concatenate([uw[i][:, C_WIDTH + hf * HALF:C_WIDTH + (hf + 1) * HALF],
                                    seqs[i]["qg"][:, hf * HALF:(hf + 1) * HALF]], axis=0)), _bf(s0))
          for (i, hf), s0 in zip(chains, s_old)]
    vn = [uw[i][:, hf * HALF:(hf + 1) * HALF] - x[0:n] for (i, hf), x in zip(chains, ws)]
    upd = [_dot_tn(_bf(seqs[i]["kg"][:, hf * HALF:(hf + 1) * HALF]), _bf(x)) for (i, hf), x in zip(chains, vn)]
    for (i, hf), s0, x in zip(chains, s_old, upd):
        state[i, hf] = s0 * seqs[i]["e_last"][:, hf * HALF:(hf + 1) * HALF] + jnp.where(smask, x, 0.0)
    for i in range(nb):
        v_new = jnp.concatenate(vn[i * halves:(i + 1) * halves], axis=1)
        o_inter = jnp.concatenate([x[n:2 * n] for x in ws[i * halves:(i + 1) * halves]], axis=1)
        oi = o_inter + _dot(_bf(qk[i]), _bd(_bf(v_new), dmask_bf, C_HEADS))
        oi = oi * lax.rsqrt(_seg_sum(oi * oi, ones_bf) * (1.0 / C_DIM) + NORM_EPS) * nw_ref[...]
        o_ref[i] = (oi * _silu(z_ref[i])).astype(o_ref.dtype)


def _gdn(p, params):
    bsz = p.shape[0]
    return _mixer_call(
        _gdn_kernel, p, [(P_C_QKV, 3 * C_WIDTH), (P_C_Z, C_WIDTH), (P_C_SMALL, SMALL)], params,
        [pltpu.VMEM((bsz, CHUNK + HALO, 3 * C_WIDTH), F32),
         pltpu.VMEM((bsz, C_WIDTH // HALF, HALF, HALF), F32)], "gdn")


def _merge_kernel(h_ref, ya_ref, yb_ref, yc_ref, wg_ref, wb_ref, o_ref):
    h = h_ref[...]
    acc = None
    for i, y_ref in enumerate((ya_ref, yb_ref, yc_ref)):
        t = _sigmoid(_dot(h, wg_ref[i])) * _dot(y_ref[...], wb_ref[i])
        acc = t if acc is None else acc + t
    o_ref[...] = acc.astype(o_ref.dtype)


def _merge(hb, ya, yb, yc, wg, wb, tm=1024, tn=256):
    m, d = hb.shape
    bw = ya.shape[1]
    tm = min(tm, m)
    yspec = pl.BlockSpec((tm, bw), lambda i, j: (i, 0))
    return pl.pallas_call(
        _merge_kernel,
        grid=(m // tm, d // tn),
        in_specs=[pl.BlockSpec((tm, d), lambda i, j: (i, 0)), yspec, yspec, yspec,
                  pl.BlockSpec((3, d, tn), lambda i, j: (0, 0, j)),
                  pl.BlockSpec((3, bw, tn), lambda i, j: (0, 0, j))],
        out_specs=pl.BlockSpec((tm, tn), lambda i, j: (i, j)),
        out_shape=jax.ShapeDtypeStruct((m, d), BF16),
        compiler_params=_cparams(("parallel", "parallel")),
        name="merge",
    )(hb, ya, yb, yc, wg, wb)


def _proj_ln_kernel(a_ref, w_ref, h_ref, g_ref, b_ref, o_ref, ob_ref, acc_ref):
    kstep = pl.program_id(1)

    @pl.when(kstep == 0)
    def _():
        acc_ref[...] = jnp.zeros(acc_ref.shape, F32)

    acc_ref[...] += _dot(a_ref[...], w_ref[...])

    @pl.when(kstep == pl.num_programs(1) - 1)
    def _():
        y = _layer_norm(DEEPNORM_ALPHA * h_ref[...] + acc_ref[...], g_ref[...], b_ref[...])
        o_ref[...] = y
        ob_ref[...] = y.astype(BF16)


def _proj_ln(a, w, h, g, b, tm, tk, name):
    m, k = a.shape
    d = w.shape[1]
    tm = min(tm, m)
    return pl.pallas_call(
        _proj_ln_kernel,
        grid=(m // tm, k // tk),
        in_specs=[pl.BlockSpec((tm, tk), lambda i, j: (i, j)),
                  pl.BlockSpec((tk, d), lambda i, j: (j, 0)),
                  pl.BlockSpec((tm, d), lambda i, j: (i, 0)),
                  pl.BlockSpec((1, d), lambda i, j: (0, 0)),
                  pl.BlockSpec((1, d), lambda i, j: (0, 0))],
        out_specs=[pl.BlockSpec((tm, d), lambda i, j: (i, 0)),
                   pl.BlockSpec((tm, d), lambda i, j: (i, 0))],
        out_shape=[jax.ShapeDtypeStruct((m, d), F32), jax.ShapeDtypeStruct((m, d), BF16)],
        scratch_shapes=[pltpu.VMEM((tm, d), F32)],
        compiler_params=_cparams(("parallel", "arbitrary")),
        name=name,
    )(a, w, h, g.reshape(1, d), b.reshape(1, d))


def _ffn_up_kernel(a_ref, wg_ref, wu_ref, cwg_ref, cwu_ref, bg_ref, bu_ref, o_ref, gbuf, ubuf, gcar, ucar,
                   *, blocks_per_seq):
    i = pl.program_id(0)
    j = pl.program_id(1)
    tm = a_ref.shape[0]
    a = a_ref[...]
    first = (i % blocks_per_seq) == 0

    def conv(w_ref, cw_ref, b_ref, buf, car):
        u = _dot(a, w_ref[...])
        buf[HALO:HALO + tm, :] = u
        buf[0:HALO, :] = jnp.where(first, 0.0, car[j])
        car[j] = u[tm - HALO:tm, :]
        return (u * cw_ref[2:3, :] + buf[HALO - 1:HALO - 1 + tm, :] * cw_ref[1:2, :]
                + buf[HALO - 2:HALO - 2 + tm, :] * cw_ref[0:1, :] + b_ref[...])

    gate = conv(wg_ref, cwg_ref, bg_ref, gbuf, gcar)
    up = conv(wu_ref, cwu_ref, bu_ref, ubuf, ucar)
    o_ref[...] = (_silu(gate) * up).astype(o_ref.dtype)


def _ffn_up(hb, w_up, conv_w, conv_b, seq, tm=1024, tn=512):
    m, d = hb.shape
    nf = w_up.shape[1] // 2
    nj = nf // tn
    tm = min(tm, seq)
    kern = functools.partial(_ffn_up_kernel, blocks_per_seq=seq // tm)
    return pl.pallas_call(
        kern,
        grid=(m // tm, nj),
        in_specs=[pl.BlockSpec((tm, d), lambda i, j: (i, 0)),
                  pl.BlockSpec((d, tn), lambda i, j: (0, j)),
                  pl.BlockSpec((d, tn), lambda i, j: (0, j + nj)),
                  pl.BlockSpec((FFN_CONV, tn), lambda i, j: (0, j)),
                  pl.BlockSpec((FFN_CONV, tn), lambda i, j: (0, j + nj)),
                  pl.BlockSpec((1, tn), lambda i, j: (0, j)),
                  pl.BlockSpec((1, tn), lambda i, j: (0, j + nj))],
        out_specs=pl.BlockSpec((tm, tn), lambda i, j: (i, j)),
        out_shape=jax.ShapeDtypeStruct((m, nf), BF16),
        scratch_shapes=[pltpu.VMEM((tm + HALO, tn), F32), pltpu.VMEM((tm + HALO, tn), F32),
                        pltpu.VMEM((nj, HALO, tn), F32), pltpu.VMEM((nj, HALO, tn), F32)],
        compiler_params=_cparams(("arbitrary", "arbitrary")),
        name="ffn_up",
    )(hb, w_up, w_up, conv_w, conv_w, conv_b, conv_b)


def _pad_cols(t, width):
    return jnp.pad(t, ((0, 0), (0, width - t.shape[1])))


def _pad_rows(t, before, total):
    return jnp.pad(t, ((before, total - before - t.shape[0]), (0, 0)))


def _relayout_w_in(w):
    a_in = 3 * A_WIDTH + A_DECAY_LORA + A_ICLR_LORA + A_GATE_LORA
    b_in = 2 * B_KEY_WIDTH + B_WIDTH + B_GATE_LORA + B_WIDTH
    wa, wb, wc = w[:, :a_in], w[:, a_in:a_in + b_in], w[:, a_in + b_in:]
    b_qkv = wb[:, :2 * B_KEY_WIDTH + B_WIDTH]
    b_gk = wb[:, 2 * B_KEY_WIDTH + B_WIDTH:2 * B_KEY_WIDTH + B_WIDTH + B_GATE_LORA]
    b_g = wb[:, 2 * B_KEY_WIDTH + B_WIDTH + B_GATE_LORA:]
    c_qkv = wc[:, :3 * C_WIDTH]
    c_ab = wc[:, 3 * C_WIDTH:3 * C_WIDTH + 2 * C_HEADS]
    c_z = wc[:, 3 * C_WIDTH + 2 * C_HEADS:]
    return jnp.concatenate([
        wa[:, :3 * A_WIDTH], b_qkv, b_g, c_qkv, c_z,
        _pad_cols(wa[:, 3 * A_WIDTH:], 256), _pad_cols(b_gk, SMALL), _pad_cols(c_ab, SMALL)], axis=1)


def kernel(x, ln_in_g, ln_in_b, w_in, mu_a, a_w0, a_w2, a_a0, a_a2, a_g2, a_kk, a_ka, a_rk, a_lnx_w, a_lnx_b,
           b_gk_w2, b_gk_b, b_norm_w, c_conv_w, c_a_log, c_dt_bias, c_norm_w, w_gate, w_branch, w_out,
           ln1_g, ln1_b, w_up, ffn_conv_w, ffn_conv_b, w_down, ln2_g, ln2_b):
    bsz, seq, d = x.shape
    m = bsz * seq
    assert seq % CHUNK == 0
    h, hb = _ln_in(x.reshape(m, d), ln_in_g, ln_in_b)
    r1 = lambda t: t.reshape(1, -1)
    for l in range(DEPTH):
        p = _matmul(hb, _relayout_w_in(w_in[l]).astype(BF16), 1024, 512, F32, "in_proj")
        p = p.reshape(bsz, seq, P_WIDTH)
        lo = A_DECAY_LORA
        ya = _rwkv(p, (
            r1(mu_a[l, :3 * A_WIDTH]), _pad_cols(r1(mu_a[l, 3 * A_WIDTH:]), 256),
            r1(a_w0[l]), _pad_rows(a_w2[l], 0, 256).astype(BF16),
            r1(a_a0[l]), _pad_rows(a_a2[l], lo, 256).astype(BF16),
            _pad_rows(a_g2[l], lo + A_ICLR_LORA, 256).astype(BF16),
            r1(a_kk[l]), r1(a_ka[l]), r1(a_rk[l]), r1(a_lnx_w[l]), r1(a_lnx_b[l])))
        yb = _gla(p, (_pad_rows(b_gk_w2[l], 0, SMALL).astype(BF16), r1(b_gk_b[l]),
                      jnp.tile(r1(b_norm_w[l]), (1, B_HEADS))))
        yc = _gdn(p, (c_conv_w[l], _pad_cols(r1(c_a_log[l]), SMALL), _pad_cols(r1(c_dt_bias[l]), SMALL),
                      jnp.tile(r1(c_norm_w[l]), (1, C_HEADS))))
        merged = _merge(hb, ya.reshape(m, -1), yb.reshape(m, -1), yc.reshape(m, -1),
                        w_gate[l].astype(BF16), w_branch[l].astype(BF16))
        h, hb = _proj_ln(merged, w_out[l].astype(BF16), h, ln1_g[l], ln1_b[l], 512, d, "out_proj")
        act = _ffn_up(hb, w_up[l].astype(BF16), ffn_conv_w[l], r1(ffn_conv_b[l]), seq)
        h, hb = _proj_ln(act, w_down[l].astype(BF16), h, ln2_g[l], ln2_b[l], 512, 1408, "ffn_down")
    return h.reshape(bsz, seq, d)
```

```python
import functools

import jax
import jax.numpy as jnp
from jax import lax
from jax.experimental import pallas as pl
from jax.experimental.pallas import tpu as pltpu

F32 = jnp.float32
BF16 = jnp.bfloat16

DEPTH = 2
A_HEADS, A_DIM, A_WIDTH = 8, 64, 512
A_DECAY_LORA, A_ICLR_LORA, A_GATE_LORA = 32, 32, 96
A_GN_EPS = 64e-5
B_HEADS, B_KEY, B_VAL = 4, 64, 128
B_KEY_WIDTH, B_WIDTH = 256, 512
B_GATE_LORA = 16
B_GATE_NORMALIZER = 16.0
C_HEADS, C_DIM, C_WIDTH = 4, 128, 512
C_CONV = 4
D_FF = 5632
FFN_CONV = 3
NORM_EPS = 1e-5
L2_EPS = 1e-6
DEEPNORM_ALPHA = (2 * DEPTH) ** 0.25

CHUNK = 64
HALO = 8
HALF = 256
SMALL = 128

P_A_MAIN = 0
P_B_MAIN = 1536
P_C_QKV = 3072
P_C_Z = 4608
P_A_SMALL = 5120
P_B_SMALL = 5376
P_C_SMALL = 5504
P_WIDTH = 5632

VMEM_LIMIT = 56 * 1024 * 1024


def _cparams(sem):
    return pltpu.CompilerParams(dimension_semantics=sem, vmem_limit_bytes=VMEM_LIMIT)


def _dot(a, b):
    return jnp.dot(a, b, preferred_element_type=F32)


def _dot_nt(a, b):
    return lax.dot_general(a, b, (((1,), (1,)), ((), ())), preferred_element_type=F32)


def _dot_tn(a, b):
    return lax.dot_general(a, b, (((0,), (0,)), ((), ())), preferred_element_type=F32)


def _bf(x):
    return x.astype(BF16)


def _layer_norm(x, g, b):
    mu = jnp.mean(x, -1, keepdims=True)
    xc = x - mu
    var = jnp.mean(xc * xc, -1, keepdims=True)
    return xc * lax.rsqrt(var + NORM_EPS) * g + b


def _softplus(x):
    return jnp.maximum(x, 0.0) + jnp.log1p(jnp.exp(-jnp.abs(x)))


def _sigmoid(x):
    return jax.nn.sigmoid(x)


def _silu(x):
    return x * jax.nn.sigmoid(x)


def _iota(shape, axis):
    return lax.broadcasted_iota(jnp.int32, shape, axis)


def _mask_bf(mask):
    return mask.astype(F32).astype(BF16)


def _block_mask(rows, cols, rblk, cblk):
    return (_iota((rows, cols), 0) // rblk) == (_iota((rows, cols), 1) // cblk)


def _bd(y, mask_bf, reps):
    return jnp.concatenate([y] * reps, axis=0) * mask_bf


def _pieces(x, count):
    out = []
    for _ in range(count - 1):
        p = _bf(x)
        out.append(p)
        x = x - p.astype(F32)
    out.append(_bf(x))
    return out


def _exact_rows(x, w_bf, count=3):
    n = x.shape[0]
    s = _dot(jnp.concatenate(_pieces(x, count), axis=0), w_bf)
    return sum(s[i * n:(i + 1) * n] for i in range(count))


def _exact_cols(w_bf, x, count=3):
    c = x.shape[1]
    s = _dot(w_bf, jnp.concatenate(_pieces(x, count), axis=1))
    return sum(s[:, i * c:(i + 1) * c] for i in range(count))


def _seg_sum(x, ones_bf):
    return jnp.concatenate([_exact_rows(x[:, i:i + HALF], ones_bf, 2) for i in range(0, x.shape[1], HALF)],
                           axis=1)


def _chunk_masks(n, width):
    return _iota((n, width), 0), _iota((n, width), 1) % n


def _tri_inv(ms, t, s, bdmask_bf):
    reps = ms[0].shape[1] // ms[0].shape[0]

    def mm(a_list, b_list):
        return [_dot(_bf(a), _bd(_bf(b), bdmask_bf, reps)) for a, b in zip(a_list, b_list)]

    def add(a_list, b_list):
        return [a + b for a, b in zip(a_list, b_list)]

    def sub(a_list, b_list):
        return [a - b for a, b in zip(a_list, b_list)]

    eye = (t == s).astype(F32)
    same16 = (t // 16) == (s // 16)
    same32 = (t // 32) == (s // 32)
    m16 = [jnp.where(same16, m, 0.0) for m in ms]
    x = [eye - m for m in m16]
    p = mm(m16, m16)
    x = add(x, mm(x, p))
    p = mm(p, p)
    x = add(x, mm(x, p))
    p = mm(p, p)
    x = add(x, mm(x, p))
    o32 = [jnp.where(same32 & jnp.logical_not(same16), m, 0.0) for m in ms]
    x = sub(x, mm(x, mm(o32, x)))
    o64 = [jnp.where(same32, 0.0, m) for m in ms]
    x = sub(x, mm(x, mm(o64, x)))
    return x


def _ln_kernel(x_ref, g_ref, b_ref, o_ref, ob_ref):
    y = _layer_norm(x_ref[...], g_ref[...], b_ref[...])
    o_ref[...] = y
    ob_ref[...] = y.astype(BF16)


def _ln_in(x, g, b, tm=256):
    m, d = x.shape
    return pl.pallas_call(
        _ln_kernel,
        grid=(m // tm,),
        in_specs=[pl.BlockSpec((tm, d), lambda i: (i, 0)),
                  pl.BlockSpec((1, d), lambda i: (0, 0)),
                  pl.BlockSpec((1, d), lambda i: (0, 0))],
        out_specs=[pl.BlockSpec((tm, d), lambda i: (i, 0)),
                   pl.BlockSpec((tm, d), lambda i: (i, 0))],
        out_shape=[jax.ShapeDtypeStruct((m, d), F32), jax.ShapeDtypeStruct((m, d), BF16)],
        compiler_params=_cparams(("parallel",)),
        name="ln_in",
    )(x, g.reshape(1, d), b.reshape(1, d))


def _mm_kernel(a_ref, w_ref, o_ref):
    o_ref[...] = _dot(a_ref[...], w_ref[...]).astype(o_ref.dtype)


def _matmul(a, w, layer, tm, tn, out_dtype, name):
    m, k = a.shape
    n = w.shape[2]
    tm = min(tm, m)
    return pl.pallas_call(
        _mm_kernel,
        grid=(m // tm, n // tn),
        in_specs=[pl.BlockSpec((tm, k), lambda i, j: (i, 0)),
                  pl.BlockSpec((None, k, tn), lambda i, j: (layer, 0, j))],
        out_specs=pl.BlockSpec((tm, tn), lambda i, j: (i, j)),
        out_shape=jax.ShapeDtypeStruct((m, n), out_dtype),
        compiler_params=_cparams(("parallel", "parallel")),
        name=name,
    )(a, w)


def _mixer_call(kern, p, blocks, params, scratch, name):
    bsz, seq, _ = p.shape
    n = CHUNK
    full = lambda arr: pl.BlockSpec(arr.shape, lambda c: (0,) * arr.ndim)
    in_specs = [pl.BlockSpec((bsz, n, width), functools.partial(lambda c, blk: (0, c, blk), blk=col // width))
                for col, width in blocks]
    return pl.pallas_call(
        kern,
        grid=(seq // n,),
        in_specs=in_specs + [full(t) for t in params],
        out_specs=pl.BlockSpec((bsz, n, 512), lambda c: (0, c, 0)),
        out_shape=jax.ShapeDtypeStruct((bsz, seq, 512), BF16),
        scratch_shapes=scratch,
        compiler_params=_cparams(("arbitrary",)),
        name=name,
    )(*([p] * len(blocks)), *params)


def _rwkv_kernel(main_ref, small_ref, mu_main_ref, mu_small_ref, w0_ref, w2_ref, a0_ref, a2_ref, g2_ref,
                 kk_ref, ka_ref, rk_ref, lnw_ref, lnb_ref, o_ref, xbuf, state):
    c = pl.program_id(0)
    nb = main_ref.shape[0]
    n = CHUNK
    wm = 3 * A_WIDTH
    reps = HALF // A_DIM
    halves = A_WIDTH // HALF

    @pl.when(c == 0)
    def _():
        xbuf[:, 0:HALO, :] = jnp.zeros((nb, HALO, xbuf.shape[2]), F32)
        state[...] = jnp.zeros(state.shape, F32)

    bdmask = _block_mask(HALF, HALF, A_DIM, A_DIM)
    bdmask_bf = _mask_bf(bdmask)
    t, s = _chunk_masks(n, HALF)
    incl = t >= s
    strict = t > s
    tri_bf = _mask_bf(_iota((n, n), 0) >= _iota((n, n), 1))

    seqs = []
    for i in range(nb):
        raw_main = main_ref[i]
        raw_small = small_ref[i]
        xbuf[i, HALO:HALO + n, 0:wm] = raw_main
        xbuf[i, HALO:HALO + n, wm:] = raw_small
        prev = xbuf[i, HALO - 1:HALO - 1 + n, :]
        xbuf[i, 0:HALO, :] = xbuf[i, n:n + HALO, :]
        x_main = raw_main + (prev[:, 0:wm] - raw_main) * mu_main_ref[...]
        x_small = raw_small + (prev[:, wm:] - raw_small) * mu_small_ref[...]
        r = x_main[:, 0:A_WIDTH]
        k = x_main[:, A_WIDTH:2 * A_WIDTH]
        v = x_main[:, 2 * A_WIDTH:3 * A_WIDTH]
        wlog = -_softplus(-(w0_ref[...] + _dot(_bf(jnp.tanh(x_small)), w2_ref[...]))) - 0.5
        logdec = -jnp.exp(wlog)
        a = _sigmoid(a0_ref[...] + _dot(_bf(x_small), a2_ref[...]))
        g = _dot(_bf(_sigmoid(x_small)), g2_ref[...])
        kkr = k * kk_ref[...]
        kk = kkr * lax.rsqrt(_seg_sum(kkr * kkr, bdmask_bf) + L2_EPS)
        k = k * (1.0 + (a - 1.0) * ka_ref[...])
        bonus = _seg_sum(r * k * rk_ref[...], bdmask_bf) * v
        cum = _exact_cols(tri_bf, logdec)
        e_neg = jnp.exp(-cum)
        e_rest = jnp.exp(cum[n - 1:n, :] - cum)
        b = kk * a
        seqs.append(dict(
            v=v, g=g, bonus=bonus,
            a_t=-kk * jnp.exp(cum - logdec),
            r_t=r * jnp.exp(cum),
            b_t=b * e_neg, k_t=k * e_neg,
            b_c=b * e_rest, k_c=k * e_rest,
            e_last=jnp.exp(cum[n - 1:n, :])))

    chains = [(i, hf) for i in range(nb) for hf in range(halves)]
    col = lambda name, i, hf: seqs[i][name][:, hf * HALF:(hf + 1) * HALF]
    s_old = [state[i, hf] for i, hf in chains]
    v_h = [col("v", i, hf) for i, hf in chains]
    bdv = [_bd(_bf(x), bdmask_bf, reps) for x in v_h]
    pr = [_dot_nt(_bf(jnp.concatenate([col("a_t", i, hf), col("r_t", i, hf)], axis=0)),
                  jnp.concatenate([_bd(_bf(col("b_t", i, hf)), bdmask_bf, reps),
                                   _bd(_bf(col("k_t", i, hf)), bdmask_bf, reps), _bf(s0)], axis=0))
          for (i, hf), s0 in zip(chains, s_old)]
    ab = [jnp.where(strict, x[0:n, 0:HALF], 0.0) for x in pr]
    rhs_u = [x[0:n, 2 * HALF:] + _dot(_bf(jnp.where(strict, x[0:n, HALF:2 * HALF], 0.0)), y)
             for x, y in zip(pr, bdv)]
    tinv = _tri_inv([-x for x in ab], t, s, bdmask_bf)
    u = [_dot(_bf(x), _bd(_bf(y), bdmask_bf, reps)) for x, y in zip(tinv, rhs_u)]
    o = [x[n:2 * n, 2 * HALF:] + _dot(
            _bf(jnp.concatenate([jnp.where(incl, x[n:2 * n, 0:HALF], 0.0),
                                 jnp.where(incl, x[n:2 * n, HALF:2 * HALF], 0.0)], axis=1)),
            jnp.concatenate([_bd(_bf(y), bdmask_bf, reps), z], axis=0))
         for x, y, z in zip(pr, u, bdv)]
    upd = [_dot_tn(_bf(jnp.concatenate([y, z], axis=0)),
                   _bf(jnp.concatenate([col("b_c", i, hf), col("k_c", i, hf)], axis=0)))
           for (i, hf), y, z in zip(chains, u, v_h)]
    for (i, hf), s0, x in zip(chains, s_old, upd):
        state[i, hf] = s0 * col("e_last", i, hf) + jnp.where(bdmask, x, 0.0)
    for i in range(nb):
        oi = jnp.concatenate(o[i * halves:(i + 1) * halves], axis=1)
        oc = oi - _seg_sum(oi, bdmask_bf) * (1.0 / A_DIM)
        var = _seg_sum(oc * oc, bdmask_bf) * (1.0 / A_DIM)
        on = oc * lax.rsqrt(var + A_GN_EPS)
        o_ref[i] = ((on * lnw_ref[...] + lnb_ref[...] + seqs[i]["bonus"]) * seqs[i]["g"]).astype(o_ref.dtype)


def _rwkv(p, params):
    bsz = p.shape[0]
    return _mixer_call(
        _rwkv_kernel, p, [(P_A_MAIN, 3 * A_WIDTH), (P_A_SMALL, 256)], params,
        [pltpu.VMEM((bsz, CHUNK + HALO, 3 * A_WIDTH + 256), F32),
         pltpu.VMEM((bsz, A_WIDTH // HALF, HALF, HALF), F32)], "rwkv7")


def _gla_kernel(main_ref, small_ref, gkw_ref, gkb_ref, nw_ref, o_ref, state):
    c = pl.program_id(0)
    nb = main_ref.shape[0]
    n = CHUNK

    @pl.when(c == 0)
    def _():
        state[...] = jnp.zeros(state.shape, F32)

    t, s = _chunk_masks(n, B_HEADS * n)
    incl = t >= s
    tri_bf = _mask_bf(_iota((n, n), 0) >= _iota((n, n), 1))
    kmask_bf = _mask_bf(_block_mask(B_HEADS * n, B_KEY_WIDTH, n, B_KEY))
    vmask_bf = _mask_bf(_block_mask(B_HEADS * n, B_WIDTH, n, B_VAL))
    smask = _block_mask(B_WIDTH, B_KEY_WIDTH, B_VAL, B_KEY)
    ones_bf = _mask_bf(_block_mask(HALF, HALF, B_VAL, B_VAL))

    qd, kd, k_c, e_last, v, g = [], [], [], [], [], []
    for i in range(nb):
        x = main_ref[i]
        q = x[:, 0:B_KEY_WIDTH] * (B_KEY ** -0.5)
        k = x[:, B_KEY_WIDTH:2 * B_KEY_WIDTH]
        v.append(x[:, 2 * B_KEY_WIDTH:2 * B_KEY_WIDTH + B_WIDTH])
        g.append(x[:, 2 * B_KEY_WIDTH + B_WIDTH:])
        gk = -_softplus(-(_dot(_bf(small_ref[i]), gkw_ref[...]) + gkb_ref[...])) / B_GATE_NORMALIZER
        bcum = _exact_cols(tri_bf, gk)
        qd.append(q * jnp.exp(bcum))
        kd.append(k * jnp.exp(-bcum))
        k_c.append(k * jnp.exp(bcum[n - 1:n, :] - bcum))
        e_last.append(jnp.exp(bcum[n - 1:n, :]))
    s_old = [state[i] for i in range(nb)]
    att = [jnp.where(incl, _dot_nt(_bf(x), _bd(_bf(y), kmask_bf, B_HEADS)), 0.0) for x, y in zip(qd, kd)]
    o = [_dot(_bf(x), _bd(_bf(y), vmask_bf, B_HEADS)) + _dot_nt(_bf(z), _bf(s0))
         for x, y, z, s0 in zip(att, v, qd, s_old)]
    upd = [_dot_tn(_bf(x), _bf(y)) for x, y in zip(v, k_c)]
    for i in range(nb):
        state[i] = s_old[i] * e_last[i] + jnp.where(smask, upd[i], 0.0)
        oi = o[i] * lax.rsqrt(_seg_sum(o[i] * o[i], ones_bf) * (1.0 / B_VAL) + NORM_EPS) * nw_ref[...]
        o_ref[i] = (oi * _silu(g[i])).astype(o_ref.dtype)


def _gla(p, params):
    bsz = p.shape[0]
    return _mixer_call(_gla_kernel, p, [(P_B_MAIN, 1536), (P_B_SMALL, SMALL)], params,
                       [pltpu.VMEM((bsz, B_WIDTH, B_KEY_WIDTH), F32)], "gla")


def _gdn_kernel(qkv_ref, z_ref, small_ref, cw_ref, alog_ref, dtb_ref, nw_ref, o_ref, xbuf, state):
    c = pl.program_id(0)
    nb = qkv_ref.shape[0]
    n = CHUNK
    hn = C_HEADS * n
    halves = C_WIDTH // HALF

    @pl.when(c == 0)
    def _():
        xbuf[:, 0:HALO, :] = jnp.zeros((nb, HALO, xbuf.shape[2]), F32)
        state[...] = jnp.zeros(state.shape, F32)

    smask = _block_mask(HALF, HALF, C_DIM, C_DIM)
    ones_bf = _mask_bf(smask)
    t, s = _chunk_masks(n, hn)
    incl = t >= s
    strict = t > s
    tri_bf = _mask_bf(_iota((n, n), 0) >= _iota((n, n), 1))
    bdmask_bf = _mask_bf(_block_mask(hn, hn, n, n))
    dmask_bf = _mask_bf(_block_mask(hn, C_WIDTH, n, C_DIM))
    lane = _iota((SMALL, C_WIDTH + hn), 1)
    head_of = jnp.where(lane < C_WIDTH, lane // C_DIM, (lane - C_WIDTH) // n)
    spread_g = _mask_bf(_iota((SMALL, C_WIDTH + hn), 0) == head_of)
    spread_b = _mask_bf(_iota((SMALL, C_WIDTH), 0) == C_HEADS + _iota((SMALL, C_WIDTH), 1) // C_DIM)

    seqs = []
    for i in range(nb):
        raw = qkv_ref[i]
        xbuf[i, HALO:HALO + n, :] = raw
        y = raw * cw_ref[C_CONV - 1:C_CONV, :]
        for j in range(1, C_CONV):
            y = y + xbuf[i, HALO - j:HALO - j + n, :] * cw_ref[C_CONV - 1 - j:C_CONV - j, :]
        xbuf[i, 0:HALO, :] = xbuf[i, n:n + HALO, :]
        qkv = _silu(y)
        q = qkv[:, 0:C_WIDTH]
        k = qkv[:, C_WIDTH:2 * C_WIDTH]
        v = qkv[:, 2 * C_WIDTH:]
        q = q * (lax.rsqrt(_seg_sum(q * q, ones_bf) + L2_EPS) * (C_DIM ** -0.5))
        k = k * lax.rsqrt(_seg_sum(k * k, ones_bf) + L2_EPS)
        small = small_ref[i]
        g_all = -jnp.exp(alog_ref[...]) * _softplus(small + dtb_ref[...])
        gam = _exact_rows(_exact_cols(tri_bf, g_all), spread_g)
        gam_w = gam[:, 0:C_WIDTH]
        gam_n = gam[:, C_WIDTH:]
        beta = _exact_rows(_sigmoid(small), spread_b)
        grow = jnp.sum(jnp.where(t == s, gam_n, 0.0), axis=0, keepdims=True)
        dec = jnp.exp(jnp.where(incl, gam_n - grow, 0.0))
        eg = jnp.exp(gam_w)
        glast = gam_w[n - 1:n, :]
        kb = k * beta
        seqs.append(dict(q=q, k=k, kb=kb, dec=dec, vb=v * beta, kbe=kb * eg, qg=q * eg,
                         kg=k * jnp.exp(glast - gam_w), e_last=jnp.exp(glast)))

    pr = [_dot_nt(_bf(jnp.concatenate([d["kb"], d["q"]], axis=0)), _bd(_bf(d["k"]), dmask_bf, C_HEADS))
          for d in seqs]
    m = [jnp.where(strict, x[0:n] * d["dec"], 0.0) for x, d in zip(pr, seqs)]
    qk = [jnp.where(incl, x[n:2 * n] * d["dec"], 0.0) for x, d in zip(pr, seqs)]
    tinv = _tri_inv(m, t, s, bdmask_bf)
    uw = [_dot(_bf(x), jnp.concatenate([_bd(_bf(d["vb"]), dmask_bf, C_HEADS),
                                        _bd(_bf(d["kbe"]), dmask_bf, C_HEADS)], axis=1))
          for x, d in zip(tinv, seqs)]
    chains = [(i, hf) for i in range(nb) for hf in range(halves)]
    s_old = [state[i, hf] for i, hf in chains]
    ws = [_dot(_bf(jnp.concatenate([uw[i][:, C_WIDTH + hf * HALF:C_WIDTH + (hf + 1) * HALF],
                                    seqs[i]["qg"][:, hf * HALF:(hf + 1) * HALF]], axis=0)), _bf(s0))
          for (i, hf), s0 in zip(chains, s_old)]
    vn = [uw[i][:, hf * HALF:(hf + 1) * HALF] - x[0:n] for (i, hf), x in zip(chains, ws)]
    upd = [_dot_tn(_bf(seqs[i]["kg"][:, hf * HALF:(hf + 1) * HALF]), _bf(x)) for (i, hf), x in zip(chains, vn)]
    for (i, hf), s0, x in zip(chains, s_old, upd):
        state[i, hf] = s0 * seqs[i]["e_last"][:, hf * HALF:(hf + 1) * HALF] + jnp.where(smask, x, 0.0)
    for i in range(nb):
        v_new = jnp.concatenate(vn[i * halves:(i + 1) * halves], axis=1)
        o_inter = jnp.concatenate([x[n:2 * n] for x in ws[i * halves:(i + 1) * halves]], axis=1)
        oi = o_inter + _dot(_bf(qk[i]), _bd(_bf(v_new), dmask_bf, C_HEADS))
        oi = oi * lax.rsqrt(_seg_sum(oi * oi, ones_bf) * (1.0 / C_DIM) + NORM_EPS) * nw_ref[...]
        o_ref[i] = (oi * _silu(z_ref[i])).astype(o_ref.dtype)


def _gdn(p, params):
    bsz = p.shape[0]
    return _mixer_call(
        _gdn_kernel, p, [(P_C_QKV, 3 * C_WIDTH), (P_C_Z, C_WIDTH), (P_C_SMALL, SMALL)], params,
        [pltpu.VMEM((bsz, CHUNK + HALO, 3 * C_WIDTH), F32),
         pltpu.VMEM((bsz, C_WIDTH // HALF, HALF, HALF), F32)], "gdn")


def _merge_kernel(h_ref, ya_ref, yb_ref, yc_ref, wg_ref, wb_ref, o_ref):
    h = h_ref[...]
    acc = None
    for i, y_ref in enumerate((ya_ref, yb_ref, yc_ref)):
        t = _sigmoid(_dot(h, wg_ref[i])) * _dot(y_ref[...], wb_ref[i])
        acc = t if acc is None else acc + t
    o_ref[...] = acc.astype(o_ref.dtype)


def _merge(hb, ya, yb, yc, wg, wb, layer, tm=1024, tn=512):
    m, d = hb.shape
    bw = ya.shape[1]
    tm = min(tm, m)
    yspec = pl.BlockSpec((tm, bw), lambda i, j: (i, 0))
    return pl.pallas_call(
        _merge_kernel,
        grid=(m // tm, d // tn),
        in_specs=[pl.BlockSpec((tm, d), lambda i, j: (i, 0)), yspec, yspec, yspec,
                  pl.BlockSpec((None, 3, d, tn), lambda i, j: (layer, 0, 0, j)),
                  pl.BlockSpec((None, 3, bw, tn), lambda i, j: (layer, 0, 0, j))],
        out_specs=pl.BlockSpec((tm, tn), lambda i, j: (i, j)),
        out_shape=jax.ShapeDtypeStruct((m, d), BF16),
        compiler_params=_cparams(("parallel", "parallel")),
        name="merge",
    )(hb, ya, yb, yc, wg, wb)


def _proj_ln_kernel(a_ref, w_ref, h_ref, g_ref, b_ref, o_ref, ob_ref):
    y = _layer_norm(DEEPNORM_ALPHA * h_ref[...] + _dot(a_ref[...], w_ref[...]), g_ref[...], b_ref[...])
    o_ref[...] = y
    ob_ref[...] = y.astype(BF16)


def _proj_ln(a, w, layer, h, g, b, tm, name):
    m, k = a.shape
    d = w.shape[2]
    tm = min(tm, m)
    return pl.pallas_call(
        _proj_ln_kernel,
        grid=(m // tm,),
        in_specs=[pl.BlockSpec((tm, k), lambda i: (i, 0)),
                  pl.BlockSpec((None, k, d), lambda i: (layer, 0, 0), pipeline_mode=pl.Buffered(1)),
                  pl.BlockSpec((tm, d), lambda i: (i, 0)),
                  pl.BlockSpec((1, d), lambda i: (0, 0)),
                  pl.BlockSpec((1, d), lambda i: (0, 0))],
        out_specs=[pl.BlockSpec((tm, d), lambda i: (i, 0)),
                   pl.BlockSpec((tm, d), lambda i: (i, 0))],
        out_shape=[jax.ShapeDtypeStruct((m, d), F32), jax.ShapeDtypeStruct((m, d), BF16)],
        compiler_params=_cparams(("parallel",)),
        name=name,
    )(a, w, h, g.reshape(1, d), b.reshape(1, d))


def _ffn_up_kernel(a_ref, wg_ref, wu_ref, cwg_ref, cwu_ref, bg_ref, bu_ref, o_ref, gcar, ucar, *, blocks_per_seq):
    i = pl.program_id(0)
    j = pl.program_id(1)
    tm = a_ref.shape[0]
    a = a_ref[...]
    first = (i % blocks_per_seq) == 0
    head_row = _iota((HALO, o_ref.shape[1]), 0)

    def shifted(u, tail, k):
        r = pltpu.roll(u, k, axis=0)
        head = jnp.where(head_row < k, pltpu.roll(tail, k, axis=0), r[0:HALO])
        return jnp.concatenate([head, r[HALO:]], axis=0)

    def conv(w_ref, cw_ref, b_ref, car):
        u = _dot(a, w_ref[...])
        tail = jnp.where(first, 0.0, car[j])
        car[j] = u[tm - HALO:tm, :]
        return (u * cw_ref[2:3, :] + shifted(u, tail, 1) * cw_ref[1:2, :]
                + shifted(u, tail, 2) * cw_ref[0:1, :] + b_ref[...])

    gate = conv(wg_ref, cwg_ref, bg_ref, gcar)
    up = conv(wu_ref, cwu_ref, bu_ref, ucar)
    o_ref[...] = (_silu(gate) * up).astype(o_ref.dtype)


def _ffn_up(hb, w_up, layer, conv_w, conv_b, seq, tm=1024, tn=512):
    m, d = hb.shape
    nf = w_up.shape[2] // 2
    nj = nf // tn
    tm = min(tm, seq)
    kern = functools.partial(_ffn_up_kernel, blocks_per_seq=seq // tm)
    return pl.pallas_call(
        kern,
        grid=(m // tm, nj),
        in_specs=[pl.BlockSpec((tm, d), lambda i, j: (i, 0)),
                  pl.BlockSpec((None, d, tn), lambda i, j: (layer, 0, j)),
                  pl.BlockSpec((None, d, tn), lambda i, j: (layer, 0, j + nj)),
                  pl.BlockSpec((FFN_CONV, tn), lambda i, j: (0, j)),
                  pl.BlockSpec((FFN_CONV, tn), lambda i, j: (0, j + nj)),
                  pl.BlockSpec((1, tn), lambda i, j: (0, j)),
                  pl.BlockSpec((1, tn), lambda i, j: (0, j + nj))],
        out_specs=pl.BlockSpec((tm, tn), lambda i, j: (i, j)),
        out_shape=jax.ShapeDtypeStruct((m, nf), BF16),
        scratch_shapes=[pltpu.VMEM((nj, HALO, tn), F32), pltpu.VMEM((nj, HALO, tn), F32)],
        compiler_params=_cparams(("arbitrary", "arbitrary")),
        name="ffn_up",
    )(hb, w_up, w_up, conv_w, conv_w, conv_b, conv_b)


def _pad_cols(t, width):
    return jnp.pad(t, ((0, 0), (0, width - t.shape[1])))


def _pad_rows(t, before, total):
    return jnp.pad(t, ((before, total - before - t.shape[0]), (0, 0)))


def _relayout_w_in(w):
    a_in = 3 * A_WIDTH + A_DECAY_LORA + A_ICLR_LORA + A_GATE_LORA
    b_in = 2 * B_KEY_WIDTH + B_WIDTH + B_GATE_LORA + B_WIDTH
    wa, wb, wc = w[..., :a_in], w[..., a_in:a_in + b_in], w[..., a_in + b_in:]
    b_qkv = wb[..., :2 * B_KEY_WIDTH + B_WIDTH]
    b_gk = wb[..., 2 * B_KEY_WIDTH + B_WIDTH:2 * B_KEY_WIDTH + B_WIDTH + B_GATE_LORA]
    b_g = wb[..., 2 * B_KEY_WIDTH + B_WIDTH + B_GATE_LORA:]
    c_qkv = wc[..., :3 * C_WIDTH]
    c_ab = wc[..., 3 * C_WIDTH:3 * C_WIDTH + 2 * C_HEADS]
    c_z = wc[..., 3 * C_WIDTH + 2 * C_HEADS:]

    def pad(t, width):
        return jnp.pad(t, [(0, 0)] * (t.ndim - 1) + [(0, width - t.shape[-1])])

    return jnp.concatenate([
        wa[..., :3 * A_WIDTH], b_qkv, b_g, c_qkv, c_z,
        pad(wa[..., 3 * A_WIDTH:], 256), pad(b_gk, SMALL), pad(c_ab, SMALL)], axis=-1)


def kernel(x, ln_in_g, ln_in_b, w_in, mu_a, a_w0, a_w2, a_a0, a_a2, a_g2, a_kk, a_ka, a_rk, a_lnx_w, a_lnx_b,
           b_gk_w2, b_gk_b, b_norm_w, c_conv_w, c_a_log, c_dt_bias, c_norm_w, w_gate, w_branch, w_out,
           ln1_g, ln1_b, w_up, ffn_conv_w, ffn_conv_b, w_down, ln2_g, ln2_b):
    bsz, seq, d = x.shape
    m = bsz * seq
    assert seq % CHUNK == 0
    h, hb = _ln_in(x.reshape(m, d), ln_in_g, ln_in_b)
    r1 = lambda t: t.reshape(1, -1)
    w_in_b = _relayout_w_in(w_in).astype(BF16)
    w_gate_b, w_branch_b, w_out_b = w_gate.astype(BF16), w_branch.astype(BF16), w_out.astype(BF16)
    w_up_b, w_down_b = w_up.astype(BF16), w_down.astype(BF16)
    for l in range(DEPTH):
        p = _matmul(hb, w_in_b, l, 512, P_WIDTH // 2, F32, "in_proj")
        p = p.reshape(bsz, seq, P_WIDTH)
        lo = A_DECAY_LORA
        ya = _rwkv(p, (
            r1(mu_a[l, :3 * A_WIDTH]), _pad_cols(r1(mu_a[l, 3 * A_WIDTH:]), 256),
            r1(a_w0[l]), _pad_rows(a_w2[l], 0, 256).astype(BF16),
            r1(a_a0[l]), _pad_rows(a_a2[l], lo, 256).astype(BF16),
            _pad_rows(a_g2[l], lo + A_ICLR_LORA, 256).astype(BF16),
            r1(a_kk[l]), r1(a_ka[l]), r1(a_rk[l]), r1(a_lnx_w[l]), r1(a_lnx_b[l])))
        yb = _gla(p, (_pad_rows(b_gk_w2[l], 0, SMALL).astype(BF16), r1(b_gk_b[l]),
                      jnp.tile(r1(b_norm_w[l]), (1, B_HEADS))))
        yc = _gdn(p, (c_conv_w[l], _pad_cols(r1(c_a_log[l]), SMALL), _pad_cols(r1(c_dt_bias[l]), SMALL),
                      jnp.tile(r1(c_norm_w[l]), (1, C_HEADS))))
        merged = _merge(hb, ya.reshape(m, -1), yb.reshape(m, -1), yc.reshape(m, -1), w_gate_b, w_branch_b, l)
        h, hb = _proj_ln(merged, w_out_b, l, h, ln1_g[l], ln1_b[l], 512, "out_proj")
        act = _ffn_up(hb, w_up_b, l, ffn_conv_w[l], r1(ffn_conv_b[l]), seq)
        h, hb = _proj_ln(act, w_down_b, l, h, ln2_g[l], ln2_b[l], 256, "ffn_down")
    return h.reshape(bsz, seq, d)
```

```python
import functools

import jax
import jax.numpy as jnp
from jax import lax
from jax.experimental import pallas as pl
from jax.experimental.pallas import tpu as pltpu

F32 = jnp.float32
BF16 = jnp.bfloat16

DEPTH = 2
A_HEADS, A_DIM, A_WIDTH = 8, 64, 512
A_DECAY_LORA, A_ICLR_LORA, A_GATE_LORA = 32, 32, 96
A_GN_EPS = 64e-5
B_HEADS, B_KEY, B_VAL = 4, 64, 128
B_KEY_WIDTH, B_WIDTH = 256, 512
B_GATE_LORA = 16
B_GATE_NORMALIZER = 16.0
C_HEADS, C_DIM, C_WIDTH = 4, 128, 512
C_CONV = 4
D_FF = 5632
FFN_CONV = 3
NORM_EPS = 1e-5
L2_EPS = 1e-6
DEEPNORM_ALPHA = (2 * DEPTH) ** 0.25

CHUNK = 64
HALO = 8
HALF = 256
SMALL = 128

P_A_MAIN = 0
P_B_MAIN = 1536
P_C_QKV = 3072
P_C_Z = 4608
P_A_SMALL = 5120
P_B_SMALL = 5376
P_C_SMALL = 5504
P_WIDTH = 5632

VMEM_LIMIT = 56 * 1024 * 1024


def _cparams(sem):
    return pltpu.CompilerParams(dimension_semantics=sem, vmem_limit_bytes=VMEM_LIMIT)


def _dot(a, b):
    return jnp.dot(a, b, preferred_element_type=F32)


def _dot_nt(a, b):
    return lax.dot_general(a, b, (((1,), (1,)), ((), ())), preferred_element_type=F32)


def _dot_tn(a, b):
    return lax.dot_general(a, b, (((0,), (0,)), ((), ())), preferred_element_type=F32)


def _bf(x):
    return x.astype(BF16)


def _layer_norm(x, g, b):
    mu = jnp.mean(x, -1, keepdims=True)
    xc = x - mu
    var = jnp.mean(xc * xc, -1, keepdims=True)
    return xc * lax.rsqrt(var + NORM_EPS) * g + b


def _softplus(x):
    return jnp.maximum(x, 0.0) + jnp.log1p(jnp.exp(-jnp.abs(x)))


def _sigmoid(x):
    return jax.nn.sigmoid(x)


def _silu(x):
    return x * jax.nn.sigmoid(x)


def _iota(shape, axis):
    return lax.broadcasted_iota(jnp.int32, shape, axis)


def _mask_bf(mask):
    return mask.astype(F32).astype(BF16)


def _block_mask(rows, cols, rblk, cblk):
    return (_iota((rows, cols), 0) // rblk) == (_iota((rows, cols), 1) // cblk)


def _bd(y, mask_bf, reps):
    return jnp.concatenate([y] * reps, axis=0) * mask_bf


def _pieces(x, count):
    out = []
    for _ in range(count - 1):
        p = _bf(x)
        out.append(p)
        x = x - p.astype(F32)
    out.append(_bf(x))
    return out


def _exact_rows(x, w_bf, count=3):
    n = x.shape[0]
    s = _dot(jnp.concatenate(_pieces(x, count), axis=0), w_bf)
    return sum(s[i * n:(i + 1) * n] for i in range(count))


def _exact_cols(w_bf, x, count=3):
    c = x.shape[1]
    s = _dot(w_bf, jnp.concatenate(_pieces(x, count), axis=1))
    return sum(s[:, i * c:(i + 1) * c] for i in range(count))


def _seg_sum(x, ones_bf):
    return jnp.concatenate([_exact_rows(x[:, i:i + HALF], ones_bf, 2) for i in range(0, x.shape[1], HALF)],
                           axis=1)


def _chunk_masks(n, width):
    return _iota((n, width), 0), _iota((n, width), 1) % n


def _tri_inv(ms, t, s, bdmask_bf):
    reps = ms[0].shape[1] // ms[0].shape[0]

    def mm(a_list, b_list):
        return [_dot(_bf(a), _bd(_bf(b), bdmask_bf, reps)) for a, b in zip(a_list, b_list)]

    def add(a_list, b_list):
        return [a + b for a, b in zip(a_list, b_list)]

    def sub(a_list, b_list):
        return [a - b for a, b in zip(a_list, b_list)]

    eye = (t == s).astype(F32)
    same16 = (t // 16) == (s // 16)
    same32 = (t // 32) == (s // 32)
    m16 = [jnp.where(same16, m, 0.0) for m in ms]
    x = [eye - m for m in m16]
    p = mm(m16, m16)
    x = add(x, mm(x, p))
    p = mm(p, p)
    x = add(x, mm(x, p))
    p = mm(p, p)
    x = add(x, mm(x, p))
    o32 = [jnp.where(same32 & jnp.logical_not(same16), m, 0.0) for m in ms]
    x = sub(x, mm(x, mm(o32, x)))
    o64 = [jnp.where(same32, 0.0, m) for m in ms]
    x = sub(x, mm(x, mm(o64, x)))
    return x


def _ln_kernel(x_ref, g_ref, b_ref, o_ref, ob_ref):
    y = _layer_norm(x_ref[...], g_ref[...], b_ref[...])
    o_ref[...] = y
    ob_ref[...] = y.astype(BF16)


def _ln_in(x, g, b, tm=256):
    m, d = x.shape
    return pl.pallas_call(
        _ln_kernel,
        grid=(m // tm,),
        in_specs=[pl.BlockSpec((tm, d), lambda i: (i, 0)),
                  pl.BlockSpec((1, d), lambda i: (0, 0)),
                  pl.BlockSpec((1, d), lambda i: (0, 0))],
        out_specs=[pl.BlockSpec((tm, d), lambda i: (i, 0)),
                   pl.BlockSpec((tm, d), lambda i: (i, 0))],
        out_shape=[jax.ShapeDtypeStruct((m, d), F32), jax.ShapeDtypeStruct((m, d), BF16)],
        compiler_params=_cparams(("parallel",)),
        name="ln_in",
    )(x, g.reshape(1, d), b.reshape(1, d))


def _mm_kernel(a_ref, w_ref, o_ref):
    o_ref[...] = _dot(a_ref[...], w_ref[...]).astype(o_ref.dtype)


def _matmul(a, w, layer, tm, tn, out_dtype, name):
    m, k = a.shape
    n = w.shape[2]
    tm = min(tm, m)
    return pl.pallas_call(
        _mm_kernel,
        grid=(n // tn, m // tm),
        in_specs=[pl.BlockSpec((tm, k), lambda j, i: (i, 0)),
                  pl.BlockSpec((None, k, tn), lambda j, i: (layer, 0, j))],
        out_specs=pl.BlockSpec((tm, tn), lambda j, i: (i, j)),
        out_shape=jax.ShapeDtypeStruct((m, n), out_dtype),
        compiler_params=_cparams(("parallel", "parallel")),
        name=name,
    )(a, w)


def _mixer_call(kern, p, blocks, params, scratch, name):
    bsz, seq, _ = p.shape
    n = CHUNK
    full = lambda arr: pl.BlockSpec(arr.shape, lambda c: (0,) * arr.ndim)
    in_specs = [pl.BlockSpec((bsz, n, width), functools.partial(lambda c, blk: (0, c, blk), blk=col // width))
                for col, width in blocks]
    return pl.pallas_call(
        kern,
        grid=(seq // n,),
        in_specs=in_specs + [full(t) for t in params],
        out_specs=pl.BlockSpec((bsz, n, 512), lambda c: (0, c, 0)),
        out_shape=jax.ShapeDtypeStruct((bsz, seq, 512), BF16),
        scratch_shapes=scratch,
        compiler_params=_cparams(("arbitrary",)),
        name=name,
    )(*([p] * len(blocks)), *params)


def _rwkv_kernel(main_ref, small_ref, mu_main_ref, mu_small_ref, w0_ref, w2_ref, a0_ref, a2_ref, g2_ref,
                 kk_ref, ka_ref, rk_ref, lnw_ref, lnb_ref, o_ref, xbuf, state):
    c = pl.program_id(0)
    nb = main_ref.shape[0]
    n = CHUNK
    wm = 3 * A_WIDTH
    reps = HALF // A_DIM
    halves = A_WIDTH // HALF

    @pl.when(c == 0)
    def _():
        xbuf[:, 0:HALO, :] = jnp.zeros((nb, HALO, xbuf.shape[2]), F32)
        state[...] = jnp.zeros(state.shape, F32)

    bdmask = _block_mask(HALF, HALF, A_DIM, A_DIM)
    bdmask_bf = _mask_bf(bdmask)
    t, s = _chunk_masks(n, HALF)
    incl = t >= s
    strict = t > s
    tri_bf = _mask_bf(_iota((n, n), 0) >= _iota((n, n), 1))

    seqs = []
    for i in range(nb):
        raw_main = main_ref[i]
        raw_small = small_ref[i]
        xbuf[i, HALO:HALO + n, 0:wm] = raw_main
        xbuf[i, HALO:HALO + n, wm:] = raw_small
        prev = xbuf[i, HALO - 1:HALO - 1 + n, :]
        xbuf[i, 0:HALO, :] = xbuf[i, n:n + HALO, :]
        x_main = raw_main + (prev[:, 0:wm] - raw_main) * mu_main_ref[...]
        x_small = raw_small + (prev[:, wm:] - raw_small) * mu_small_ref[...]
        r = x_main[:, 0:A_WIDTH]
        k = x_main[:, A_WIDTH:2 * A_WIDTH]
        v = x_main[:, 2 * A_WIDTH:3 * A_WIDTH]
        wlog = -_softplus(-(w0_ref[...] + _dot(_bf(jnp.tanh(x_small)), w2_ref[...]))) - 0.5
        logdec = -jnp.exp(wlog)
        a = _sigmoid(a0_ref[...] + _dot(_bf(x_small), a2_ref[...]))
        g = _dot(_bf(_sigmoid(x_small)), g2_ref[...])
        kkr = k * kk_ref[...]
        kk = kkr * lax.rsqrt(_seg_sum(kkr * kkr, bdmask_bf) + L2_EPS)
        k = k * (1.0 + (a - 1.0) * ka_ref[...])
        bonus = _seg_sum(r * k * rk_ref[...], bdmask_bf) * v
        cum = _exact_cols(tri_bf, logdec)
        e_neg = jnp.exp(-cum)
        e_rest = jnp.exp(cum[n - 1:n, :] - cum)
        b = kk * a
        seqs.append(dict(
            v=v, g=g, bonus=bonus,
            a_t=-kk * jnp.exp(cum - logdec),
            r_t=r * jnp.exp(cum),
            b_t=b * e_neg, k_t=k * e_neg,
            b_c=b * e_rest, k_c=k * e_rest,
            e_last=jnp.exp(cum[n - 1:n, :])))

    chains = [(i, hf) for i in range(nb) for hf in range(halves)]
    col = lambda name, i, hf: seqs[i][name][:, hf * HALF:(hf + 1) * HALF]
    s_old = [state[i, hf] for i, hf in chains]
    v_h = [col("v", i, hf) for i, hf in chains]
    bdv = [_bd(_bf(x), bdmask_bf, reps) for x in v_h]
    pr = [_dot_nt(_bf(jnp.concatenate([col("a_t", i, hf), col("r_t", i, hf)], axis=0)),
                  jnp.concatenate([_bd(_bf(col("b_t", i, hf)), bdmask_bf, reps),
                                   _bd(_bf(col("k_t", i, hf)), bdmask_bf, reps), _bf(s0)], axis=0))
          for (i, hf), s0 in zip(chains, s_old)]
    ab = [jnp.where(strict, x[0:n, 0:HALF], 0.0) for x in pr]
    rhs_u = [x[0:n, 2 * HALF:] + _dot(_bf(jnp.where(strict, x[0:n, HALF:2 * HALF], 0.0)), y)
             for x, y in zip(pr, bdv)]
    tinv = _tri_inv([-x for x in ab], t, s, bdmask_bf)
    u = [_dot(_bf(x), _bd(_bf(y), bdmask_bf, reps)) for x, y in zip(tinv, rhs_u)]
    o = [x[n:2 * n, 2 * HALF:] + _dot(
            _bf(jnp.concatenate([jnp.where(incl, x[n:2 * n, 0:HALF], 0.0),
                                 jnp.where(incl, x[n:2 * n, HALF:2 * HALF], 0.0)], axis=1)),
            jnp.concatenate([_bd(_bf(y), bdmask_bf, reps), z], axis=0))
         for x, y, z in zip(pr, u, bdv)]
    upd = [_dot_tn(_bf(jnp.concatenate([y, z], axis=0)),
                   _bf(jnp.concatenate([col("b_c", i, hf), col("k_c", i, hf)], axis=0)))
           for (i, hf), y, z in zip(chains, u, v_h)]
    for (i, hf), s0, x in zip(chains, s_old, upd):
        state[i, hf] = s0 * col("e_last", i, hf) + jnp.where(bdmask, x, 0.0)
    for i in range(nb):
        oi = jnp.concatenate(o[i * halves:(i + 1) * halves], axis=1)
        oc = oi - _seg_sum(oi, bdmask_bf) * (1.0 / A_DIM)
        var = _seg_sum(oc * oc, bdmask_bf) * (1.0 / A_DIM)
        on = oc * lax.rsqrt(var + A_GN_EPS)
        o_ref[i] = ((on * lnw_ref[...] + lnb_ref[...] + seqs[i]["bonus"]) * seqs[i]["g"]).astype(o_ref.dtype)


def _rwkv(p, params):
    bsz = p.shape[0]
    return _mixer_call(
        _rwkv_kernel, p, [(P_A_MAIN, 3 * A_WIDTH), (P_A_SMALL, 256)], params,
        [pltpu.VMEM((bsz, CHUNK + HALO, 3 * A_WIDTH + 256), F32),
         pltpu.VMEM((bsz, A_WIDTH // HALF, HALF, HALF), F32)], "rwkv7")


def _gla_kernel(main_ref, small_ref, gkw_ref, gkb_ref, nw_ref, o_ref, state):
    c = pl.program_id(0)
    nb = main_ref.shape[0]
    n = CHUNK

    @pl.when(c == 0)
    def _():
        state[...] = jnp.zeros(state.shape, F32)

    t, s = _chunk_masks(n, B_HEADS * n)
    incl = t >= s
    tri_bf = _mask_bf(_iota((n, n), 0) >= _iota((n, n), 1))
    kmask_bf = _mask_bf(_block_mask(B_HEADS * n, B_KEY_WIDTH, n, B_KEY))
    vmask_bf = _mask_bf(_block_mask(B_HEADS * n, B_WIDTH, n, B_VAL))
    smask = _block_mask(B_WIDTH, B_KEY_WIDTH, B_VAL, B_KEY)
    ones_bf = _mask_bf(_block_mask(HALF, HALF, B_VAL, B_VAL))

    qd, kd, k_c, e_last, v, g = [], [], [], [], [], []
    for i in range(nb):
        x = main_ref[i]
        q = x[:, 0:B_KEY_WIDTH] * (B_KEY ** -0.5)
        k = x[:, B_KEY_WIDTH:2 * B_KEY_WIDTH]
        v.append(x[:, 2 * B_KEY_WIDTH:2 * B_KEY_WIDTH + B_WIDTH])
        g.append(x[:, 2 * B_KEY_WIDTH + B_WIDTH:])
        gk = -_softplus(-(_dot(_bf(small_ref[i]), gkw_ref[...]) + gkb_ref[...])) / B_GATE_NORMALIZER
        bcum = _exact_cols(tri_bf, gk)
        qd.append(q * jnp.exp(bcum))
        kd.append(k * jnp.exp(-bcum))
        k_c.append(k * jnp.exp(bcum[n - 1:n, :] - bcum))
        e_last.append(jnp.exp(bcum[n - 1:n, :]))
    s_old = [state[i] for i in range(nb)]
    att = [jnp.where(incl, _dot_nt(_bf(x), _bd(_bf(y), kmask_bf, B_HEADS)), 0.0) for x, y in zip(qd, kd)]
    o = [_dot(_bf(x), _bd(_bf(y), vmask_bf, B_HEADS)) + _dot_nt(_bf(z), _bf(s0))
         for x, y, z, s0 in zip(att, v, qd, s_old)]
    upd = [_dot_tn(_bf(x), _bf(y)) for x, y in zip(v, k_c)]
    for i in range(nb):
        state[i] = s_old[i] * e_last[i] + jnp.where(smask, upd[i], 0.0)
        oi = o[i] * lax.rsqrt(_seg_sum(o[i] * o[i], ones_bf) * (1.0 / B_VAL) + NORM_EPS) * nw_ref[...]
        o_ref[i] = (oi * _silu(g[i])).astype(o_ref.dtype)


def _gla(p, params):
    bsz = p.shape[0]
    return _mixer_call(_gla_kernel, p, [(P_B_MAIN, 1536), (P_B_SMALL, SMALL)], params,
                       [pltpu.VMEM((bsz, B_WIDTH, B_KEY_WIDTH), F32)], "gla")


def _gdn_kernel(qkv_ref, z_ref, small_ref, cw_ref, alog_ref, dtb_ref, nw_ref, o_ref, xbuf, state):
    c = pl.program_id(0)
    nb = qkv_ref.shape[0]
    n = CHUNK
    hn = C_HEADS * n
    halves = C_WIDTH // HALF

    @pl.when(c == 0)
    def _():
        xbuf[:, 0:HALO, :] = jnp.zeros((nb, HALO, xbuf.shape[2]), F32)
        state[...] = jnp.zeros(state.shape, F32)

    smask = _block_mask(HALF, HALF, C_DIM, C_DIM)
    ones_bf = _mask_bf(smask)
    t, s = _chunk_masks(n, hn)
    incl = t >= s
    strict = t > s
    tri_bf = _mask_bf(_iota((n, n), 0) >= _iota((n, n), 1))
    bdmask_bf = _mask_bf(_block_mask(hn, hn, n, n))
    dmask_bf = _mask_bf(_block_mask(hn, C_WIDTH, n, C_DIM))
    lane = _iota((SMALL, C_WIDTH + hn), 1)
    head_of = jnp.where(lane < C_WIDTH, lane // C_DIM, (lane - C_WIDTH) // n)
    spread_g = _mask_bf(_iota((SMALL, C_WIDTH + hn), 0) == head_of)
    spread_b = _mask_bf(_iota((SMALL, C_WIDTH), 0) == C_HEADS + _iota((SMALL, C_WIDTH), 1) // C_DIM)

    seqs = []
    for i in range(nb):
        raw = qkv_ref[i]
        xbuf[i, HALO:HALO + n, :] = raw
        y = raw * cw_ref[C_CONV - 1:C_CONV, :]
        for j in range(1, C_CONV):
            y = y + xbuf[i, HALO - j:HALO - j + n, :] * cw_ref[C_CONV - 1 - j:C_CONV - j, :]
        xbuf[i, 0:HALO, :] = xbuf[i, n:n + HALO, :]
        qkv = _silu(y)
        q = qkv[:, 0:C_WIDTH]
        k = qkv[:, C_WIDTH:2 * C_WIDTH]
        v = qkv[:, 2 * C_WIDTH:]
        q = q * (lax.rsqrt(_seg_sum(q * q, ones_bf) + L2_EPS) * (C_DIM ** -0.5))
        k = k * lax.rsqrt(_seg_sum(k * k, ones_bf) + L2_EPS)
        small = small_ref[i]
        g_all = -jnp.exp(alog_ref[...]) * _softplus(small + dtb_ref[...])
        gam = _exact_rows(_exact_cols(tri_bf, g_all), spread_g)
        gam_w = gam[:, 0:C_WIDTH]
        gam_n = gam[:, C_WIDTH:]
        beta = _exact_rows(_sigmoid(small), spread_b)
        grow = jnp.sum(jnp.where(t == s, gam_n, 0.0), axis=0, keepdims=True)
        dec = jnp.exp(jnp.where(incl, gam_n - grow, 0.0))
        eg = jnp.exp(gam_w)
        glast = gam_w[n - 1:n, :]
        kb = k * beta
        seqs.append(dict(q=q, k=k, kb=kb, dec=dec, vb=v * beta, kbe=kb * eg, qg=q * eg,
                         kg=k * jnp.exp(glast - gam_w), e_last=jnp.exp(glast)))

    pr = [_dot_nt(_bf(jnp.concatenate([d["kb"], d["q"]], axis=0)), _bd(_bf(d["k"]), dmask_bf, C_HEADS))
          for d in seqs]
    m = [jnp.where(strict, x[0:n] * d["dec"], 0.0) for x, d in zip(pr, seqs)]
    qk = [jnp.where(incl, x[n:2 * n] * d["dec"], 0.0) for x, d in zip(pr, seqs)]
    tinv = _tri_inv(m, t, s, bdmask_bf)
    uw = [_dot(_bf(x), jnp.concatenate([_bd(_bf(d["vb"]), dmask_bf, C_HEADS),
                                        _bd(_bf(d["kbe"]), dmask_bf, C_HEADS)], axis=1))
          for x, d in zip(tinv, seqs)]
    chains = [(i, hf) for i in range(nb) for hf in range(halves)]
    s_old = [state[i, hf] for i, hf in chains]
    ws = [_dot(_bf(jnp.concatenate([uw[i][:, C_WIDTH + hf * HALF:C_WIDTH + (hf + 1) * HALF],
                                    seqs[i]["qg"][:, hf * HALF:(hf + 1) * HALF]], axis=0)), _bf(s0))
          for (i, hf), s0 in zip(chains, s_old)]
    vn = [uw[i][:, hf * HALF:(hf + 1) * HALF] - x[0:n] for (i, hf), x in zip(chains, ws)]
    upd = [_dot_tn(_bf(seqs[i]["kg"][:, hf * HALF:(hf + 1) * HALF]), _bf(x)) for (i, hf), x in zip(chains, vn)]
    for (i, hf), s0, x in zip(chains, s_old, upd):
        state[i, hf] = s0 * seqs[i]["e_last"][:, hf * HALF:(hf + 1) * HALF] + jnp.where(smask, x, 0.0)
    for i in range(nb):
        v_new = jnp.concatenate(vn[i * halves:(i + 1) * halves], axis=1)
        o_inter = jnp.concatenate([x[n:2 * n] for x in ws[i * halves:(i + 1) * halves]], axis=1)
        oi = o_inter + _dot(_bf(qk[i]), _bd(_bf(v_new), dmask_bf, C_HEADS))
        oi = oi * lax.rsqrt(_seg_sum(oi * oi, ones_bf) * (1.0 / C_DIM) + NORM_EPS) * nw_ref[...]
        o_ref[i] = (oi * _silu(z_ref[i])).astype(o_ref.dtype)


def _gdn(p, params):
    bsz = p.shape[0]
    return _mixer_call(
        _gdn_kernel, p, [(P_C_QKV, 3 * C_WIDTH), (P_C_Z, C_WIDTH), (P_C_SMALL, SMALL)], params,
        [pltpu.VMEM((bsz, CHUNK + HALO, 3 * C_WIDTH), F32),
         pltpu.VMEM((bsz, C_WIDTH // HALF, HALF, HALF), F32)], "gdn")


def _merge_kernel(h_ref, ya_ref, yb_ref, yc_ref, wg_ref, wb_ref, o_ref):
    h = h_ref[...]
    acc = None
    for i, y_ref in enumerate((ya_ref, yb_ref, yc_ref)):
        t = _sigmoid(_dot(h, wg_ref[i])) * _dot(y_ref[...], wb_ref[i])
        acc = t if acc is None else acc + t
    o_ref[...] = acc.astype(o_ref.dtype)


def _merge(hb, ya, yb, yc, wg, wb, layer, tm=1024, tn=512):
    m, d = hb.shape
    bw = ya.shape[1]
    tm = min(tm, m)
    yspec = pl.BlockSpec((tm, bw), lambda i, j: (i, 0))
    return pl.pallas_call(
        _merge_kernel,
        grid=(m // tm, d // tn),
        in_specs=[pl.BlockSpec((tm, d), lambda i, j: (i, 0)), yspec, yspec, yspec,
                  pl.BlockSpec((None, 3, d, tn), lambda i, j: (layer, 0, 0, j)),
                  pl.BlockSpec((None, 3, bw, tn), lambda i, j: (layer, 0, 0, j))],
        out_specs=pl.BlockSpec((tm, tn), lambda i, j: (i, j)),
        out_shape=jax.ShapeDtypeStruct((m, d), BF16),
        compiler_params=_cparams(("parallel", "parallel")),
        name="merge",
    )(hb, ya, yb, yc, wg, wb)


def _proj_ln_kernel(a_ref, w_ref, h_ref, g_ref, b_ref, o_ref, ob_ref):
    y = _layer_norm(DEEPNORM_ALPHA * h_ref[...] + _dot(a_ref[...], w_ref[...]), g_ref[...], b_ref[...])
    o_ref[...] = y
    ob_ref[...] = y.astype(BF16)


def _proj_ln(a, w, layer, h, g, b, tm, name):
    m, k = a.shape
    d = w.shape[2]
    tm = min(tm, m)
    return pl.pallas_call(
        _proj_ln_kernel,
        grid=(m // tm,),
        in_specs=[pl.BlockSpec((tm, k), lambda i: (i, 0)),
                  pl.BlockSpec((None, k, d), lambda i: (layer, 0, 0), pipeline_mode=pl.Buffered(1)),
                  pl.BlockSpec((tm, d), lambda i: (i, 0)),
                  pl.BlockSpec((1, d), lambda i: (0, 0)),
                  pl.BlockSpec((1, d), lambda i: (0, 0))],
        out_specs=[pl.BlockSpec((tm, d), lambda i: (i, 0)),
                   pl.BlockSpec((tm, d), lambda i: (i, 0))],
        out_shape=[jax.ShapeDtypeStruct((m, d), F32), jax.ShapeDtypeStruct((m, d), BF16)],
        compiler_params=_cparams(("parallel",)),
        name=name,
    )(a, w, h, g.reshape(1, d), b.reshape(1, d))


def _ffn_up_kernel(a_ref, wg_ref, wu_ref, cwg_ref, cwu_ref, bg_ref, bu_ref, o_ref, gcar, ucar, *, blocks_per_seq):
    i = pl.program_id(0)
    j = pl.program_id(1)
    tm = a_ref.shape[0]
    a = a_ref[...]
    first = (i % blocks_per_seq) == 0
    head_row = _iota((HALO, o_ref.shape[1]), 0)

    def shifted(u, tail, k):
        r = pltpu.roll(u, k, axis=0)
        head = jnp.where(head_row < k, pltpu.roll(tail, k, axis=0), r[0:HALO])
        return jnp.concatenate([head, r[HALO:]], axis=0)

    def conv(w_ref, cw_ref, b_ref, car):
        u = _dot(a, w_ref[...])
        tail = jnp.where(first, 0.0, car[j])
        car[j] = u[tm - HALO:tm, :]
        return (u * cw_ref[2:3, :] + shifted(u, tail, 1) * cw_ref[1:2, :]
                + shifted(u, tail, 2) * cw_ref[0:1, :] + b_ref[...])

    gate = conv(wg_ref, cwg_ref, bg_ref, gcar)
    up = conv(wu_ref, cwu_ref, bu_ref, ucar)
    o_ref[...] = (_silu(gate) * up).astype(o_ref.dtype)


def _ffn_up(hb, w_up, layer, conv_w, conv_b, seq, tm=1024, tn=512):
    m, d = hb.shape
    nf = w_up.shape[2] // 2
    nj = nf // tn
    tm = min(tm, seq)
    kern = functools.partial(_ffn_up_kernel, blocks_per_seq=seq // tm)
    return pl.pallas_call(
        kern,
        grid=(m // tm, nj),
        in_specs=[pl.BlockSpec((tm, d), lambda i, j: (i, 0)),
                  pl.BlockSpec((None, d, tn), lambda i, j: (layer, 0, j)),
                  pl.BlockSpec((None, d, tn), lambda i, j: (layer, 0, j + nj)),
                  pl.BlockSpec((FFN_CONV, tn), lambda i, j: (0, j)),
                  pl.BlockSpec((FFN_CONV, tn), lambda i, j: (0, j + nj)),
                  pl.BlockSpec((1, tn), lambda i, j: (0, j)),
                  pl.BlockSpec((1, tn), lambda i, j: (0, j + nj))],
        out_specs=pl.BlockSpec((tm, tn), lambda i, j: (i, j)),
        out_shape=jax.ShapeDtypeStruct((m, nf), BF16),
        scratch_shapes=[pltpu.VMEM((nj, HALO, tn), F32), pltpu.VMEM((nj, HALO, tn), F32)],
        compiler_params=_cparams(("arbitrary", "arbitrary")),
        name="ffn_up",
    )(hb, w_up, w_up, conv_w, conv_w, conv_b, conv_b)


def _pad_cols(t, width):
    return jnp.pad(t, ((0, 0), (0, width - t.shape[1])))


def _pad_rows(t, before, total):
    return jnp.pad(t, ((before, total - before - t.shape[0]), (0, 0)))


def _relayout_w_in(w):
    a_in = 3 * A_WIDTH + A_DECAY_LORA + A_ICLR_LORA + A_GATE_LORA
    b_in = 2 * B_KEY_WIDTH + B_WIDTH + B_GATE_LORA + B_WIDTH
    wa, wb, wc = w[..., :a_in], w[..., a_in:a_in + b_in], w[..., a_in + b_in:]
    b_qkv = wb[..., :2 * B_KEY_WIDTH + B_WIDTH]
    b_gk = wb[..., 2 * B_KEY_WIDTH + B_WIDTH:2 * B_KEY_WIDTH + B_WIDTH + B_GATE_LORA]
    b_g = wb[..., 2 * B_KEY_WIDTH + B_WIDTH + B_GATE_LORA:]
    c_qkv = wc[..., :3 * C_WIDTH]
    c_ab = wc[..., 3 * C_WIDTH:3 * C_WIDTH + 2 * C_HEADS]
    c_z = wc[..., 3 * C_WIDTH + 2 * C_HEADS:]

    def pad(t, width):
        return jnp.pad(t, [(0, 0)] * (t.ndim - 1) + [(0, width - t.shape[-1])])

    return jnp.concatenate([
        wa[..., :3 * A_WIDTH], b_qkv, b_g, c_qkv, c_z,
        pad(wa[..., 3 * A_WIDTH:], 256), pad(b_gk, SMALL), pad(c_ab, SMALL)], axis=-1)


def kernel(x, ln_in_g, ln_in_b, w_in, mu_a, a_w0, a_w2, a_a0, a_a2, a_g2, a_kk, a_ka, a_rk, a_lnx_w, a_lnx_b,
           b_gk_w2, b_gk_b, b_norm_w, c_conv_w, c_a_log, c_dt_bias, c_norm_w, w_gate, w_branch, w_out,
           ln1_g, ln1_b, w_up, ffn_conv_w, ffn_conv_b, w_down, ln2_g, ln2_b):
    bsz, seq, d = x.shape
    m = bsz * seq
    assert seq % CHUNK == 0
    h, hb = _ln_in(x.reshape(m, d), ln_in_g, ln_in_b)
    r1 = lambda t: t.reshape(1, -1)
    w_in_b = _relayout_w_in(w_in.astype(BF16))
    w_gate_b, w_branch_b, w_out_b = w_gate.astype(BF16), w_branch.astype(BF16), w_out.astype(BF16)
    w_up_b, w_down_b = w_up.astype(BF16), w_down.astype(BF16)
    for l in range(DEPTH):
        p = _matmul(hb, w_in_b, l, 512, P_WIDTH // 2, F32, "in_proj")
        p = p.reshape(bsz, seq, P_WIDTH)
        lo = A_DECAY_LORA
        ya = _rwkv(p, (
            r1(mu_a[l, :3 * A_WIDTH]), _pad_cols(r1(mu_a[l, 3 * A_WIDTH:]), 256),
            r1(a_w0[l]), _pad_rows(a_w2[l], 0, 256).astype(BF16),
            r1(a_a0[l]), _pad_rows(a_a2[l], lo, 256).astype(BF16),
            _pad_rows(a_g2[l], lo + A_ICLR_LORA, 256).astype(BF16),
            r1(a_kk[l]), r1(a_ka[l]), r1(a_rk[l]), r1(a_lnx_w[l]), r1(a_lnx_b[l])))
        yb = _gla(p, (_pad_rows(b_gk_w2[l], 0, SMALL).astype(BF16), r1(b_gk_b[l]),
                      jnp.tile(r1(b_norm_w[l]), (1, B_HEADS))))
        yc = _gdn(p, (c_conv_w[l], _pad_cols(r1(c_a_log[l]), SMALL), _pad_cols(r1(c_dt_bias[l]), SMALL),
                      jnp.tile(r1(c_norm_w[l]), (1, C_HEADS))))
        merged = _merge(hb, ya.reshape(m, -1), yb.reshape(m, -1), yc.reshape(m, -1), w_gate_b, w_branch_b, l)
        h, hb = _proj_ln(merged, w_out_b, l, h, ln1_g[l], ln1_b[l], 512, "out_proj")
        act = _ffn_up(hb, w_up_b, l, ffn_conv_w[l], r1(ffn_conv_b[l]), seq)
        h, hb = _proj_ln(act, w_down_b, l, h, ln2_g[l], ln2_b[l], 256, "ffn_down")
    return h.reshape(bsz, seq, d)
```

```python
import functools

import jax
import jax.numpy as jnp
from jax import lax
from jax.experimental import pallas as pl
from jax.experimental.pallas import tpu as pltpu

F32 = jnp.float32
BF16 = jnp.bfloat16

DEPTH = 2
A_HEADS, A_DIM, A_WIDTH = 8, 64, 512
A_DECAY_LORA, A_ICLR_LORA, A_GATE_LORA = 32, 32, 96
A_GN_EPS = 64e-5
B_HEADS, B_KEY, B_VAL = 4, 64, 128
B_KEY_WIDTH, B_WIDTH = 256, 512
B_GATE_LORA = 16
B_GATE_NORMALIZER = 16.0
C_HEADS, C_DIM, C_WIDTH = 4, 128, 512
C_CONV = 4
D_FF = 5632
FFN_CONV = 3
NORM_EPS = 1e-5
L2_EPS = 1e-6
DEEPNORM_ALPHA = (2 * DEPTH) ** 0.25

CHUNK = 64
HALO = 8
HALF = 256
SMALL = 128

P_A_MAIN = 0
P_B_MAIN = 1536
P_C_QKV = 3072
P_C_Z = 4608
P_A_SMALL = 5120
P_B_SMALL = 5376
P_C_SMALL = 5504
P_WIDTH = 5632

VMEM_LIMIT = 56 * 1024 * 1024


def _cparams(sem):
    return pltpu.CompilerParams(dimension_semantics=sem, vmem_limit_bytes=VMEM_LIMIT)


def _dot(a, b):
    return jnp.dot(a, b, preferred_element_type=F32)


def _dot_nt(a, b):
    return lax.dot_general(a, b, (((1,), (1,)), ((), ())), preferred_element_type=F32)


def _dot_tn(a, b):
    return lax.dot_general(a, b, (((0,), (0,)), ((), ())), preferred_element_type=F32)


def _bf(x):
    return x.astype(BF16)


def _layer_norm(x, g, b):
    mu = jnp.mean(x, -1, keepdims=True)
    xc = x - mu
    var = jnp.mean(xc * xc, -1, keepdims=True)
    return xc * lax.rsqrt(var + NORM_EPS) * g + b


def _softplus(x):
    return jnp.maximum(x, 0.0) + jnp.log1p(jnp.exp(-jnp.abs(x)))


def _sigmoid(x):
    return jax.nn.sigmoid(x)


def _silu(x):
    return x * jax.nn.sigmoid(x)


def _iota(shape, axis):
    return lax.broadcasted_iota(jnp.int32, shape, axis)


def _mask_bf(mask):
    return mask.astype(F32).astype(BF16)


def _block_mask(rows, cols, rblk, cblk):
    return (_iota((rows, cols), 0) // rblk) == (_iota((rows, cols), 1) // cblk)


def _bd(y, mask_bf, reps):
    return jnp.concatenate([y] * reps, axis=0) * mask_bf


def _pieces(x, count):
    out = []
    for _ in range(count - 1):
        p = _bf(x)
        out.append(p)
        x = x - p.astype(F32)
    out.append(_bf(x))
    return out


def _exact_rows(x, w_bf, count=3):
    n = x.shape[0]
    s = _dot(jnp.concatenate(_pieces(x, count), axis=0), w_bf)
    return sum(s[i * n:(i + 1) * n] for i in range(count))


def _exact_cols(w_bf, x, count=3):
    c = x.shape[1]
    s = _dot(w_bf, jnp.concatenate(_pieces(x, count), axis=1))
    return sum(s[:, i * c:(i + 1) * c] for i in range(count))


def _seg_sum(x, ones_bf):
    return jnp.concatenate([_exact_rows(x[:, i:i + HALF], ones_bf, 2) for i in range(0, x.shape[1], HALF)],
                           axis=1)


def _chunk_masks(n, width):
    return _iota((n, width), 0), _iota((n, width), 1) % n


def _tri_inv(ms, t, s, bdmask_bf):
    reps = ms[0].shape[1] // ms[0].shape[0]

    def mm(a_list, b_list):
        return [_dot(_bf(a), _bd(_bf(b), bdmask_bf, reps)) for a, b in zip(a_list, b_list)]

    def add(a_list, b_list):
        return [a + b for a, b in zip(a_list, b_list)]

    def sub(a_list, b_list):
        return [a - b for a, b in zip(a_list, b_list)]

    eye = (t == s).astype(F32)
    same16 = (t // 16) == (s // 16)
    same32 = (t // 32) == (s // 32)
    m16 = [jnp.where(same16, m, 0.0) for m in ms]
    x = [eye - m for m in m16]
    p = mm(m16, m16)
    x = add(x, mm(x, p))
    p = mm(p, p)
    x = add(x, mm(x, p))
    p = mm(p, p)
    x = add(x, mm(x, p))
    o32 = [jnp.where(same32 & jnp.logical_not(same16), m, 0.0) for m in ms]
    x = sub(x, mm(x, mm(o32, x)))
    o64 = [jnp.where(same32, 0.0, m) for m in ms]
    x = sub(x, mm(x, mm(o64, x)))
    return x


def _ln_kernel(x_ref, g_ref, b_ref, o_ref, ob_ref):
    y = _layer_norm(x_ref[...], g_ref[...], b_ref[...])
    o_ref[...] = y
    ob_ref[...] = y.astype(BF16)


def _ln_in(x, g, b, tm=256):
    m, d = x.shape
    return pl.pallas_call(
        _ln_kernel,
        grid=(m // tm,),
        in_specs=[pl.BlockSpec((tm, d), lambda i: (i, 0)),
                  pl.BlockSpec((1, d), lambda i: (0, 0)),
                  pl.BlockSpec((1, d), lambda i: (0, 0))],
        out_specs=[pl.BlockSpec((tm, d), lambda i: (i, 0)),
                   pl.BlockSpec((tm, d), lambda i: (i, 0))],
        out_shape=[jax.ShapeDtypeStruct((m, d), F32), jax.ShapeDtypeStruct((m, d), BF16)],
        compiler_params=_cparams(("parallel",)),
        name="ln_in",
    )(x, g.reshape(1, d), b.reshape(1, d))


def _mm_kernel(a_ref, w_ref, o_ref):
    o_ref[...] = _dot(a_ref[...], w_ref[...]).astype(o_ref.dtype)


def _matmul(a, w, layer, tm, tn, out_dtype, name):
    m, k = a.shape
    n = w.shape[2]
    tm = min(tm, m)
    return pl.pallas_call(
        _mm_kernel,
        grid=(n // tn, m // tm),
        in_specs=[pl.BlockSpec((tm, k), lambda j, i: (i, 0)),
                  pl.BlockSpec((None, k, tn), lambda j, i: (layer, 0, j))],
        out_specs=pl.BlockSpec((tm, tn), lambda j, i: (i, j)),
        out_shape=jax.ShapeDtypeStruct((m, n), out_dtype),
        compiler_params=_cparams(("parallel", "parallel")),
        name=name,
    )(a, w)


def _mixer_call(kern, p, blocks, params, scratch, name):
    bsz, seq, _ = p.shape
    n = CHUNK
    full = lambda arr: pl.BlockSpec(arr.shape, lambda c: (0,) * arr.ndim)
    in_specs = [pl.BlockSpec((bsz, n, width), functools.partial(lambda c, blk: (0, c, blk), blk=col // width))
                for col, width in blocks]
    return pl.pallas_call(
        kern,
        grid=(seq // n,),
        in_specs=in_specs + [full(t) for t in params],
        out_specs=pl.BlockSpec((bsz, n, 512), lambda c: (0, c, 0)),
        out_shape=jax.ShapeDtypeStruct((bsz, seq, 512), BF16),
        scratch_shapes=scratch,
        compiler_params=_cparams(("arbitrary",)),
        name=name,
    )(*([p] * len(blocks)), *params)


def _rwkv_kernel(main_ref, small_ref, mu_main_ref, mu_small_ref, w0_ref, w2_ref, a0_ref, a2_ref, g2_ref,
                 kk_ref, ka_ref, rk_ref, lnw_ref, lnb_ref, o_ref, xbuf, state):
    c = pl.program_id(0)
    nb = main_ref.shape[0]
    n = CHUNK
    wm = 3 * A_WIDTH
    reps = HALF // A_DIM
    halves = A_WIDTH // HALF

    @pl.when(c == 0)
    def _():
        xbuf[:, 0:HALO, :] = jnp.zeros((nb, HALO, xbuf.shape[2]), F32)
        state[...] = jnp.zeros(state.shape, F32)

    bdmask = _block_mask(HALF, HALF, A_DIM, A_DIM)
    bdmask_bf = _mask_bf(bdmask)
    t, s = _chunk_masks(n, HALF)
    incl = t >= s
    strict = t > s
    tri_bf = _mask_bf(_iota((n, n), 0) >= _iota((n, n), 1))

    seqs = []
    for i in range(nb):
        raw_main = main_ref[i]
        raw_small = small_ref[i]
        xbuf[i, HALO:HALO + n, 0:wm] = raw_main
        xbuf[i, HALO:HALO + n, wm:] = raw_small
        prev = xbuf[i, HALO - 1:HALO - 1 + n, :]
        xbuf[i, 0:HALO, :] = xbuf[i, n:n + HALO, :]
        x_main = raw_main + (prev[:, 0:wm] - raw_main) * mu_main_ref[...]
        x_small = raw_small + (prev[:, wm:] - raw_small) * mu_small_ref[...]
        r = x_main[:, 0:A_WIDTH]
        k = x_main[:, A_WIDTH:2 * A_WIDTH]
        v = x_main[:, 2 * A_WIDTH:3 * A_WIDTH]
        wlog = -_softplus(-(w0_ref[...] + _dot(_bf(jnp.tanh(x_small)), w2_ref[...]))) - 0.5
        logdec = -jnp.exp(wlog)
        a = _sigmoid(a0_ref[...] + _dot(_bf(x_small), a2_ref[...]))
        g = _dot(_bf(_sigmoid(x_small)), g2_ref[...])
        kkr = k * kk_ref[...]
        kk = kkr * lax.rsqrt(_seg_sum(kkr * kkr, bdmask_bf) + L2_EPS)
        k = k * (1.0 + (a - 1.0) * ka_ref[...])
        bonus = _seg_sum(r * k * rk_ref[...], bdmask_bf) * v
        cum = _exact_cols(tri_bf, logdec)
        e_neg = jnp.exp(-cum)
        e_rest = jnp.exp(cum[n - 1:n, :] - cum)
        b = kk * a
        seqs.append(dict(
            v=v, g=g, bonus=bonus,
            a_t=-kk * jnp.exp(cum - logdec),
            r_t=r * jnp.exp(cum),
            b_t=b * e_neg, k_t=k * e_neg,
            b_c=b * e_rest, k_c=k * e_rest,
            e_last=jnp.exp(cum[n - 1:n, :])))

    chains = [(i, hf) for i in range(nb) for hf in range(halves)]
    col = lambda name, i, hf: seqs[i][name][:, hf * HALF:(hf + 1) * HALF]
    s_old = [state[i, hf] for i, hf in chains]
    v_h = [col("v", i, hf) for i, hf in chains]
    bdv = [_bd(_bf(x), bdmask_bf, reps) for x in v_h]
    pr = [_dot_nt(_bf(jnp.concatenate([col("a_t", i, hf), col("r_t", i, hf)], axis=0)),
                  jnp.concatenate([_bd(_bf(col("b_t", i, hf)), bdmask_bf, reps),
                                   _bd(_bf(col("k_t", i, hf)), bdmask_bf, reps), _bf(s0)], axis=0))
          for (i, hf), s0 in zip(chains, s_old)]
    ab = [jnp.where(strict, x[0:n, 0:HALF], 0.0) for x in pr]
    rhs_u = [x[0:n, 2 * HALF:] + _dot(_bf(jnp.where(strict, x[0:n, HALF:2 * HALF], 0.0)), y)
             for x, y in zip(pr, bdv)]
    tinv = _tri_inv([-x for x in ab], t, s, bdmask_bf)
    u = [_dot(_bf(x), _bd(_bf(y), bdmask_bf, reps)) for x, y in zip(tinv, rhs_u)]
    o = [x[n:2 * n, 2 * HALF:] + _dot(
            _bf(jnp.concatenate([jnp.where(incl, x[n:2 * n, 0:HALF], 0.0),
                                 jnp.where(incl, x[n:2 * n, HALF:2 * HALF], 0.0)], axis=1)),
            jnp.concatenate([_bd(_bf(y), bdmask_bf, reps), z], axis=0))
         for x, y, z in zip(pr, u, bdv)]
    upd = [_dot_tn(_bf(jnp.concatenate([y, z], axis=0)),
                   _bf(jnp.concatenate([col("b_c", i, hf), col("k_c", i, hf)], axis=0)))
           for (i, hf), y, z in zip(chains, u, v_h)]
    for (i, hf), s0, x in zip(chains, s_old, upd):
        state[i, hf] = s0 * col("e_last", i, hf) + jnp.where(bdmask, x, 0.0)
    for i in range(nb):
        oi = jnp.concatenate(o[i * halves:(i + 1) * halves], axis=1)
        oc = oi - _seg_sum(oi, bdmask_bf) * (1.0 / A_DIM)
        var = _seg_sum(oc * oc, bdmask_bf) * (1.0 / A_DIM)
        on = oc * lax.rsqrt(var + A_GN_EPS)
        o_ref[i] = ((on * lnw_ref[...] + lnb_ref[...] + seqs[i]["bonus"]) * seqs[i]["g"]).astype(o_ref.dtype)


def _rwkv(p, params):
    bsz = p.shape[0]
    return _mixer_call(
        _rwkv_kernel, p, [(P_A_MAIN, 3 * A_WIDTH), (P_A_SMALL, 256)], params,
        [pltpu.VMEM((bsz, CHUNK + HALO, 3 * A_WIDTH + 256), F32),
         pltpu.VMEM((bsz, A_WIDTH // HALF, HALF, HALF), F32)], "rwkv7")


def _gla_kernel(main_ref, small_ref, gkw_ref, gkb_ref, nw_ref, o_ref, state):
    c = pl.program_id(0)
    nb = main_ref.shape[0]
    n = CHUNK

    @pl.when(c == 0)
    def _():
        state[...] = jnp.zeros(state.shape, F32)

    t, s = _chunk_masks(n, B_HEADS * n)
    incl = t >= s
    tri_bf = _mask_bf(_iota((n, n), 0) >= _iota((n, n), 1))
    kmask_bf = _mask_bf(_block_mask(B_HEADS * n, B_KEY_WIDTH, n, B_KEY))
    vmask_bf = _mask_bf(_block_mask(B_HEADS * n, B_WIDTH, n, B_VAL))
    smask = _block_mask(B_WIDTH, B_KEY_WIDTH, B_VAL, B_KEY)
    ones_bf = _mask_bf(_block_mask(HALF, HALF, B_VAL, B_VAL))

    qd, kd, k_c, e_last, v, g = [], [], [], [], [], []
    for i in range(nb):
        x = main_ref[i]
        q = x[:, 0:B_KEY_WIDTH] * (B_KEY ** -0.5)
        k = x[:, B_KEY_WIDTH:2 * B_KEY_WIDTH]
        v.append(x[:, 2 * B_KEY_WIDTH:2 * B_KEY_WIDTH + B_WIDTH])
        g.append(x[:, 2 * B_KEY_WIDTH + B_WIDTH:])
        gk = -_softplus(-(_dot(_bf(small_ref[i]), gkw_ref[...]) + gkb_ref[...])) / B_GATE_NORMALIZER
        bcum = _exact_cols(tri_bf, gk)
        qd.append(q * jnp.exp(bcum))
        kd.append(k * jnp.exp(-bcum))
        k_c.append(k * jnp.exp(bcum[n - 1:n, :] - bcum))
        e_last.append(jnp.exp(bcum[n - 1:n, :]))
    s_old = [state[i] for i in range(nb)]
    att = [jnp.where(incl, _dot_nt(_bf(x), _bd(_bf(y), kmask_bf, B_HEADS)), 0.0) for x, y in zip(qd, kd)]
    o = [_dot(_bf(x), _bd(_bf(y), vmask_bf, B_HEADS)) + _dot_nt(_bf(z), _bf(s0))
         for x, y, z, s0 in zip(att, v, qd, s_old)]
    upd = [_dot_tn(_bf(x), _bf(y)) for x, y in zip(v, k_c)]
    for i in range(nb):
        state[i] = s_old[i] * e_last[i] + jnp.where(smask, upd[i], 0.0)
        oi = o[i] * lax.rsqrt(_seg_sum(o[i] * o[i], ones_bf) * (1.0 / B_VAL) + NORM_EPS) * nw_ref[...]
        o_ref[i] = (oi * _silu(g[i])).astype(o_ref.dtype)


def _gla(p, params):
    bsz = p.shape[0]
    return _mixer_call(_gla_kernel, p, [(P_B_MAIN, 1536), (P_B_SMALL, SMALL)], params,
                       [pltpu.VMEM((bsz, B_WIDTH, B_KEY_WIDTH), F32)], "gla")


def _gdn_kernel(qkv_ref, z_ref, small_ref, cw_ref, alog_ref, dtb_ref, nw_ref, o_ref, xbuf, state):
    c = pl.program_id(0)
    nb = qkv_ref.shape[0]
    n = CHUNK
    hn = C_HEADS * n
    halves = C_WIDTH // HALF

    @pl.when(c == 0)
    def _():
        xbuf[:, 0:HALO, :] = jnp.zeros((nb, HALO, xbuf.shape[2]), F32)
        state[...] = jnp.zeros(state.shape, F32)

    smask = _block_mask(HALF, HALF, C_DIM, C_DIM)
    ones_bf = _mask_bf(smask)
    t, s = _chunk_masks(n, hn)
    incl = t >= s
    strict = t > s
    tri_bf = _mask_bf(_iota((n, n), 0) >= _iota((n, n), 1))
    bdmask_bf = _mask_bf(_block_mask(hn, hn, n, n))
    dmask_bf = _mask_bf(_block_mask(hn, C_WIDTH, n, C_DIM))
    lane = _iota((SMALL, C_WIDTH + hn), 1)
    head_of = jnp.where(lane < C_WIDTH, lane // C_DIM, (lane - C_WIDTH) // n)
    spread_g = _mask_bf(_iota((SMALL, C_WIDTH + hn), 0) == head_of)
    spread_b = _mask_bf(_iota((SMALL, C_WIDTH), 0) == C_HEADS + _iota((SMALL, C_WIDTH), 1) // C_DIM)

    seqs = []
    for i in range(nb):
        raw = qkv_ref[i]
        xbuf[i, HALO:HALO + n, :] = raw
        y = raw * cw_ref[C_CONV - 1:C_CONV, :]
        for j in range(1, C_CONV):
            y = y + xbuf[i, HALO - j:HALO - j + n, :] * cw_ref[C_CONV - 1 - j:C_CONV - j, :]
        xbuf[i, 0:HALO, :] = xbuf[i, n:n + HALO, :]
        qkv = _silu(y)
        q = qkv[:, 0:C_WIDTH]
        k = qkv[:, C_WIDTH:2 * C_WIDTH]
        v = qkv[:, 2 * C_WIDTH:]
        q = q * (lax.rsqrt(_seg_sum(q * q, ones_bf) + L2_EPS) * (C_DIM ** -0.5))
        k = k * lax.rsqrt(_seg_sum(k * k, ones_bf) + L2_EPS)
        small = small_ref[i]
        g_all = -jnp.exp(alog_ref[...]) * _softplus(small + dtb_ref[...])
        gam = _exact_rows(_exact_cols(tri_bf, g_all), spread_g)
        gam_w = gam[:, 0:C_WIDTH]
        gam_n = gam[:, C_WIDTH:]
        beta = _exact_rows(_sigmoid(small), spread_b)
        grow = jnp.sum(jnp.where(t == s, gam_n, 0.0), axis=0, keepdims=True)
        dec = jnp.exp(jnp.where(incl, gam_n - grow, 0.0))
        eg = jnp.exp(gam_w)
        glast = gam_w[n - 1:n, :]
        kb = k * beta
        seqs.append(dict(q=q, k=k, kb=kb, dec=dec, vb=v * beta, kbe=kb * eg, qg=q * eg,
                         kg=k * jnp.exp(glast - gam_w), e_last=jnp.exp(glast)))

    pr = [_dot_nt(_bf(jnp.concatenate([d["kb"], d["q"]], axis=0)), _bd(_bf(d["k"]), dmask_bf, C_HEADS))
          for d in seqs]
    m = [jnp.where(strict, x[0:n] * d["dec"], 0.0) for x, d in zip(pr, seqs)]
    qk = [jnp.where(incl, x[n:2 * n] * d["dec"], 0.0) for x, d in zip(pr, seqs)]
    tinv = _tri_inv(m, t, s, bdmask_bf)
    uw = [_dot(_bf(x), jnp.concatenate([_bd(_bf(d["vb"]), dmask_bf, C_HEADS),
                                        _bd(_bf(d["kbe"]), dmask_bf, C_HEADS)], axis=1))
          for x, d in zip(tinv, seqs)]
    chains = [(i, hf) for i in range(nb) for hf in range(halves)]
    s_old = [state[i, hf] for i, hf in chains]
    ws = [_dot(_bf(jnp.concatenate([uw[i][:, C_WIDTH + hf * HALF:C_WIDTH + (hf + 1) * HALF],
                                    seqs[i]["qg"][:, hf * HALF:(hf + 1) * HALF]], axis=0)), _bf(s0))
          for (i, hf), s0 in zip(chains, s_old)]
    vn = [uw[i][:, hf * HALF:(hf + 1) * HALF] - x[0:n] for (i, hf), x in zip(chains, ws)]
    upd = [_dot_tn(_bf(seqs[i]["kg"][:, hf * HALF:(hf + 1) * HALF]), _bf(x)) for (i, hf), x in zip(chains, vn)]
    for (i, hf), s0, x in zip(chains, s_old, upd):
        state[i, hf] = s0 * seqs[i]["e_last"][:, hf * HALF:(hf + 1) * HALF] + jnp.where(smask, x, 0.0)
    for i in range(nb):
        v_new = jnp.concatenate(vn[i * halves:(i + 1) * halves], axis=1)
        o_inter = jnp.concatenate([x[n:2 * n] for x in ws[i * halves:(i + 1) * halves]], axis=1)
        oi = o_inter + _dot(_bf(qk[i]), _bd(_bf(v_new), dmask_bf, C_HEADS))
        oi = oi * lax.rsqrt(_seg_sum(oi * oi, ones_bf) * (1.0 / C_DIM) + NORM_EPS) * nw_ref[...]
        o_ref[i] = (oi * _silu(z_ref[i])).astype(o_ref.dtype)


def _gdn(p, params):
    bsz = p.shape[0]
    return _mixer_call(
        _gdn_kernel, p, [(P_C_QKV, 3 * C_WIDTH), (P_C_Z, C_WIDTH), (P_C_SMALL, SMALL)], params,
        [pltpu.VMEM((bsz, CHUNK + HALO, 3 * C_WIDTH), F32),
         pltpu.VMEM((bsz, C_WIDTH // HALF, HALF, HALF), F32)], "gdn")


def _merge_kernel(h_ref, ya_ref, yb_ref, yc_ref, wg_ref, wb_ref, o_ref):
    h = h_ref[...]
    acc = None
    for i, y_ref in enumerate((ya_ref, yb_ref, yc_ref)):
        t = _sigmoid(_dot(h, wg_ref[i])) * _dot(y_ref[...], wb_ref[i])
        acc = t if acc is None else acc + t
    o_ref[...] = acc.astype(o_ref.dtype)


def _merge(hb, ya, yb, yc, wg, wb, layer, tm=1024, tn=512):
    m, d = hb.shape
    bw = ya.shape[1]
    tm = min(tm, m)
    yspec = pl.BlockSpec((tm, bw), lambda i, j: (i, 0))
    return pl.pallas_call(
        _merge_kernel,
        grid=(m // tm, d // tn),
        in_specs=[pl.BlockSpec((tm, d), lambda i, j: (i, 0)), yspec, yspec, yspec,
                  pl.BlockSpec((None, 3, d, tn), lambda i, j: (layer, 0, 0, j)),
                  pl.BlockSpec((None, 3, bw, tn), lambda i, j: (layer, 0, 0, j))],
        out_specs=pl.BlockSpec((tm, tn), lambda i, j: (i, j)),
        out_shape=jax.ShapeDtypeStruct((m, d), BF16),
        compiler_params=_cparams(("parallel", "parallel")),
        name="merge",
    )(hb, ya, yb, yc, wg, wb)


def _proj_ln_kernel(a_ref, w_ref, h_ref, g_ref, b_ref, o_ref, ob_ref):
    y = _layer_norm(DEEPNORM_ALPHA * h_ref[...] + _dot(a_ref[...], w_ref[...]), g_ref[...], b_ref[...])
    o_ref[...] = y
    ob_ref[...] = y.astype(BF16)


def _proj_ln(a, w, layer, h, g, b, tm, name):
    m, k = a.shape
    d = w.shape[2]
    tm = min(tm, m)
    return pl.pallas_call(
        _proj_ln_kernel,
        grid=(m // tm,),
        in_specs=[pl.BlockSpec((tm, k), lambda i: (i, 0)),
                  pl.BlockSpec((None, k, d), lambda i: (layer, 0, 0), pipeline_mode=pl.Buffered(1)),
                  pl.BlockSpec((tm, d), lambda i: (i, 0)),
                  pl.BlockSpec((1, d), lambda i: (0, 0)),
                  pl.BlockSpec((1, d), lambda i: (0, 0))],
        out_specs=[pl.BlockSpec((tm, d), lambda i: (i, 0)),
                   pl.BlockSpec((tm, d), lambda i: (i, 0))],
        out_shape=[jax.ShapeDtypeStruct((m, d), F32), jax.ShapeDtypeStruct((m, d), BF16)],
        compiler_params=_cparams(("parallel",)),
        name=name,
    )(a, w, h, g.reshape(1, d), b.reshape(1, d))


def _ffn_up_kernel(a_ref, wg_ref, wu_ref, cwg_ref, cwu_ref, bg_ref, bu_ref, o_ref, gcar, ucar, *, blocks_per_seq):
    i = pl.program_id(0)
    j = pl.program_id(1)
    tm = a_ref.shape[0]
    a = a_ref[...]
    first = (i % blocks_per_seq) == 0
    head_row = _iota((HALO, o_ref.shape[1]), 0)

    def shifted(u, tail, k):
        r = pltpu.roll(u, k, axis=0)
        head = jnp.where(head_row < k, pltpu.roll(tail, k, axis=0), r[0:HALO])
        return jnp.concatenate([head, r[HALO:]], axis=0)

    def conv(w_ref, cw_ref, b_ref, car):
        u = _dot(a, w_ref[...])
        tail = jnp.where(first, 0.0, car[j])
        car[j] = u[tm - HALO:tm, :]
        return (u * cw_ref[2:3, :] + shifted(u, tail, 1) * cw_ref[1:2, :]
                + shifted(u, tail, 2) * cw_ref[0:1, :] + b_ref[...])

    gate = conv(wg_ref, cwg_ref, bg_ref, gcar)
    up = conv(wu_ref, cwu_ref, bu_ref, ucar)
    o_ref[...] = (_silu(gate) * up).astype(o_ref.dtype)


def _ffn_up(hb, w_gate_half, w_up_half, layer, conv_w, conv_b, seq, tm=1024, tn=512):
    m, d = hb.shape
    nf = w_gate_half.shape[2]
    nj = nf // tn
    tm = min(tm, seq)
    kern = functools.partial(_ffn_up_kernel, blocks_per_seq=seq // tm)
    wspec = pl.BlockSpec((None, d, tn), lambda i, j: (layer, 0, j))
    cspec = pl.BlockSpec((FFN_CONV, tn), lambda i, j: (0, j))
    bspec = pl.BlockSpec((1, tn), lambda i, j: (0, j))
    return pl.pallas_call(
        kern,
        grid=(m // tm, nj),
        in_specs=[pl.BlockSpec((tm, d), lambda i, j: (i, 0)), wspec, wspec, cspec, cspec, bspec, bspec],
        out_specs=pl.BlockSpec((tm, tn), lambda i, j: (i, j)),
        out_shape=jax.ShapeDtypeStruct((m, nf), BF16),
        scratch_shapes=[pltpu.VMEM((nj, HALO, tn), F32), pltpu.VMEM((nj, HALO, tn), F32)],
        compiler_params=_cparams(("arbitrary", "arbitrary")),
        name="ffn_up",
    )(hb, w_gate_half, w_up_half, conv_w[:, :nf], conv_w[:, nf:], conv_b[:, :nf], conv_b[:, nf:])


def _pad_cols(t, width):
    return jnp.pad(t, ((0, 0), (0, width - t.shape[1])))


def _pad_rows(t, before, total):
    return jnp.pad(t, ((before, total - before - t.shape[0]), (0, 0)))


def _relayout_w_in(w):
    a_in = 3 * A_WIDTH + A_DECAY_LORA + A_ICLR_LORA + A_GATE_LORA
    b_in = 2 * B_KEY_WIDTH + B_WIDTH + B_GATE_LORA + B_WIDTH
    wa, wb, wc = w[..., :a_in], w[..., a_in:a_in + b_in], w[..., a_in + b_in:]
    b_qkv = wb[..., :2 * B_KEY_WIDTH + B_WIDTH]
    b_gk = wb[..., 2 * B_KEY_WIDTH + B_WIDTH:2 * B_KEY_WIDTH + B_WIDTH + B_GATE_LORA]
    b_g = wb[..., 2 * B_KEY_WIDTH + B_WIDTH + B_GATE_LORA:]
    c_qkv = wc[..., :3 * C_WIDTH]
    c_ab = wc[..., 3 * C_WIDTH:3 * C_WIDTH + 2 * C_HEADS]
    c_z = wc[..., 3 * C_WIDTH + 2 * C_HEADS:]

    def pad(t, width):
        return jnp.pad(t, [(0, 0)] * (t.ndim - 1) + [(0, width - t.shape[-1])])

    return jnp.concatenate([
        wa[..., :3 * A_WIDTH], b_qkv, b_g, c_qkv, c_z,
        pad(wa[..., 3 * A_WIDTH:], 256), pad(b_gk, SMALL), pad(c_ab, SMALL)], axis=-1)


def kernel(x, ln_in_g, ln_in_b, w_in, mu_a, a_w0, a_w2, a_a0, a_a2, a_g2, a_kk, a_ka, a_rk, a_lnx_w, a_lnx_b,
           b_gk_w2, b_gk_b, b_norm_w, c_conv_w, c_a_log, c_dt_bias, c_norm_w, w_gate, w_branch, w_out,
           ln1_g, ln1_b, w_up, ffn_conv_w, ffn_conv_b, w_down, ln2_g, ln2_b):
    bsz, seq, d = x.shape
    m = bsz * seq
    assert seq % CHUNK == 0
    h, hb = _ln_in(x.reshape(m, d), ln_in_g, ln_in_b)
    r1 = lambda t: t.reshape(1, -1)
    w_in_b = _relayout_w_in(w_in.astype(BF16))
    w_gate_b, w_branch_b, w_out_b = w_gate.astype(BF16), w_branch.astype(BF16), w_out.astype(BF16)
    w_ffg_b, w_ffu_b = w_up[..., :D_FF].astype(BF16), w_up[..., D_FF:].astype(BF16)
    w_down_b = w_down.astype(BF16)
    for l in range(DEPTH):
        p = _matmul(hb, w_in_b, l, 512, P_WIDTH // 2, F32, "in_proj")
        p = p.reshape(bsz, seq, P_WIDTH)
        lo = A_DECAY_LORA
        ya = _rwkv(p, (
            r1(mu_a[l, :3 * A_WIDTH]), _pad_cols(r1(mu_a[l, 3 * A_WIDTH:]), 256),
            r1(a_w0[l]), _pad_rows(a_w2[l], 0, 256).astype(BF16),
            r1(a_a0[l]), _pad_rows(a_a2[l], lo, 256).astype(BF16),
            _pad_rows(a_g2[l], lo + A_ICLR_LORA, 256).astype(BF16),
            r1(a_kk[l]), r1(a_ka[l]), r1(a_rk[l]), r1(a_lnx_w[l]), r1(a_lnx_b[l])))
        yb = _gla(p, (_pad_rows(b_gk_w2[l], 0, SMALL).astype(BF16), r1(b_gk_b[l]),
                      jnp.tile(r1(b_norm_w[l]), (1, B_HEADS))))
        yc = _gdn(p, (c_conv_w[l], _pad_cols(r1(c_a_log[l]), SMALL), _pad_cols(r1(c_dt_bias[l]), SMALL),
                      jnp.tile(r1(c_norm_w[l]), (1, C_HEADS))))
        merged = _merge(hb, ya.reshape(m, -1), yb.reshape(m, -1), yc.reshape(m, -1), w_gate_b, w_branch_b, l)
        h, hb = _proj_ln(merged, w_out_b, l, h, ln1_g[l], ln1_b[l], 512, "out_proj")
        act = _ffn_up(hb, w_ffg_b, w_ffu_b, l, ffn_conv_w[l], r1(ffn_conv_b[l]), seq)
        h, hb = _proj_ln(act, w_down_b, l, h, ln2_g[l], ln2_b[l], 256, "ffn_down")
    return h.reshape(bsz, seq, d)
```

```python
import functools

import jax
import jax.numpy as jnp
from jax import lax
from jax.experimental import pallas as pl
from jax.experimental.pallas import tpu as pltpu

F32 = jnp.float32
BF16 = jnp.bfloat16

DEPTH = 2
A_HEADS, A_DIM, A_WIDTH = 8, 64, 512
A_DECAY_LORA, A_ICLR_LORA, A_GATE_LORA = 32, 32, 96
A_GN_EPS = 64e-5
B_HEADS, B_KEY, B_VAL = 4, 64, 128
B_KEY_WIDTH, B_WIDTH = 256, 512
B_GATE_LORA = 16
B_GATE_NORMALIZER = 16.0
C_HEADS, C_DIM, C_WIDTH = 4, 128, 512
C_CONV = 4
D_FF = 5632
FFN_CONV = 3
NORM_EPS = 1e-5
L2_EPS = 1e-6
DEEPNORM_ALPHA = (2 * DEPTH) ** 0.25

CHUNK = 64
HALO = 8
HALF = 256
SMALL = 128

P_A_MAIN = 0
P_B_MAIN = 1536
P_C_QKV = 3072
P_C_Z = 4608
P_A_SMALL = 5120
P_B_SMALL = 5376
P_C_SMALL = 5504
P_WIDTH = 5632

VMEM_LIMIT = 56 * 1024 * 1024


def _cparams(sem):
    return pltpu.CompilerParams(dimension_semantics=sem, vmem_limit_bytes=VMEM_LIMIT)


def _dot(a, b):
    return jnp.dot(a, b, preferred_element_type=F32)


def _dot_nt(a, b):
    return lax.dot_general(a, b, (((1,), (1,)), ((), ())), preferred_element_type=F32)


def _dot_tn(a, b):
    return lax.dot_general(a, b, (((0,), (0,)), ((), ())), preferred_element_type=F32)


def _bf(x):
    return x.astype(BF16)


def _layer_norm(x, g, b):
    mu = jnp.mean(x, -1, keepdims=True)
    xc = x - mu
    var = jnp.mean(xc * xc, -1, keepdims=True)
    return xc * lax.rsqrt(var + NORM_EPS) * g + b


def _softplus(x):
    return jnp.maximum(x, 0.0) + jnp.log1p(jnp.exp(-jnp.abs(x)))


def _sigmoid(x):
    return jax.nn.sigmoid(x)


def _silu(x):
    return x * jax.nn.sigmoid(x)


def _iota(shape, axis):
    return lax.broadcasted_iota(jnp.int32, shape, axis)


def _mask_bf(mask):
    return mask.astype(F32).astype(BF16)


def _block_mask(rows, cols, rblk, cblk):
    return (_iota((rows, cols), 0) // rblk) == (_iota((rows, cols), 1) // cblk)


def _bd(y, mask_bf, reps):
    return jnp.concatenate([y] * reps, axis=0) * mask_bf


def _pieces(x, count):
    out = []
    for _ in range(count - 1):
        p = _bf(x)
        out.append(p)
        x = x - p.astype(F32)
    out.append(_bf(x))
    return out


def _exact_rows(x, w_bf, count=3):
    n = x.shape[0]
    s = _dot(jnp.concatenate(_pieces(x, count), axis=0), w_bf)
    return sum(s[i * n:(i + 1) * n] for i in range(count))


def _exact_cols(w_bf, x, count=3):
    c = x.shape[1]
    s = _dot(w_bf, jnp.concatenate(_pieces(x, count), axis=1))
    return sum(s[:, i * c:(i + 1) * c] for i in range(count))


def _seg_sum(x, ones_bf):
    return jnp.concatenate([_exact_rows(x[:, i:i + HALF], ones_bf, 2) for i in range(0, x.shape[1], HALF)],
                           axis=1)


def _chunk_masks(n, width):
    return _iota((n, width), 0), _iota((n, width), 1) % n


def _tri_inv(ms, t, s, bdmask_bf):
    reps = ms[0].shape[1] // ms[0].shape[0]

    def mm(a_list, b_list):
        return [_dot(_bf(a), _bd(_bf(b), bdmask_bf, reps)) for a, b in zip(a_list, b_list)]

    def add(a_list, b_list):
        return [a + b for a, b in zip(a_list, b_list)]

    def sub(a_list, b_list):
        return [a - b for a, b in zip(a_list, b_list)]

    n = ms[0].shape[0]
    eye = (t == s).astype(F32)
    same16 = (t // 16) == (s // 16)
    same32 = (t // 32) == (s // 32)
    m16 = [jnp.where(same16, m, 0.0) for m in ms]
    x = [eye - m for m in m16]
    p = mm(m16, m16)
    for _ in range(2):
        xp = mm([jnp.concatenate([a, b], axis=0) for a, b in zip(x, p)], p)
        x = [a + b[0:n] for a, b in zip(x, xp)]
        p = [b[n:2 * n] for b in xp]
    x = add(x, mm(x, p))
    o32 = [jnp.where(same32 & jnp.logical_not(same16), m, 0.0) for m in ms]
    x = sub(x, mm(x, mm(o32, x)))
    o64 = [jnp.where(same32, 0.0, m) for m in ms]
    x = sub(x, mm(x, mm(o64, x)))
    return x


def _ln_kernel(x_ref, g_ref, b_ref, o_ref, ob_ref):
    y = _layer_norm(x_ref[...], g_ref[...], b_ref[...])
    o_ref[...] = y
    ob_ref[...] = y.astype(BF16)


def _ln_in(x, g, b, tm=256):
    m, d = x.shape
    return pl.pallas_call(
        _ln_kernel,
        grid=(m // tm,),
        in_specs=[pl.BlockSpec((tm, d), lambda i: (i, 0)),
                  pl.BlockSpec((1, d), lambda i: (0, 0)),
                  pl.BlockSpec((1, d), lambda i: (0, 0))],
        out_specs=[pl.BlockSpec((tm, d), lambda i: (i, 0)),
                   pl.BlockSpec((tm, d), lambda i: (i, 0))],
        out_shape=[jax.ShapeDtypeStruct((m, d), F32), jax.ShapeDtypeStruct((m, d), BF16)],
        compiler_params=_cparams(("parallel",)),
        name="ln_in",
    )(x, g.reshape(1, d), b.reshape(1, d))


def _mm_kernel(a_ref, w_ref, o_ref):
    o_ref[...] = _dot(a_ref[...], w_ref[...]).astype(o_ref.dtype)


def _matmul(a, w, layer, tm, tn, out_dtype, name):
    m, k = a.shape
    n = w.shape[2]
    tm = min(tm, m)
    return pl.pallas_call(
        _mm_kernel,
        grid=(n // tn, m // tm),
        in_specs=[pl.BlockSpec((tm, k), lambda j, i: (i, 0)),
                  pl.BlockSpec((None, k, tn), lambda j, i: (layer, 0, j))],
        out_specs=pl.BlockSpec((tm, tn), lambda j, i: (i, j)),
        out_shape=jax.ShapeDtypeStruct((m, n), out_dtype),
        compiler_params=_cparams(("parallel", "parallel")),
        name=name,
    )(a, w)


def _mixer_call(kern, p, blocks, params, scratch, name):
    bsz, seq, _ = p.shape
    n = CHUNK
    full = lambda arr: pl.BlockSpec(arr.shape, lambda c: (0,) * arr.ndim)
    in_specs = [pl.BlockSpec((bsz, n, width), functools.partial(lambda c, blk: (0, c, blk), blk=col // width))
                for col, width in blocks]
    return pl.pallas_call(
        kern,
        grid=(seq // n,),
        in_specs=in_specs + [full(t) for t in params],
        out_specs=pl.BlockSpec((bsz, n, 512), lambda c: (0, c, 0)),
        out_shape=jax.ShapeDtypeStruct((bsz, seq, 512), BF16),
        scratch_shapes=scratch,
        compiler_params=_cparams(("arbitrary",)),
        name=name,
    )(*([p] * len(blocks)), *params)


def _rwkv_kernel(main_ref, small_ref, mu_main_ref, mu_small_ref, w0_ref, w2_ref, a0_ref, a2_ref, g2_ref,
                 kk_ref, ka_ref, rk_ref, lnw_ref, lnb_ref, o_ref, xbuf, state):
    c = pl.program_id(0)
    nb = main_ref.shape[0]
    n = CHUNK
    wm = 3 * A_WIDTH
    reps = HALF // A_DIM
    halves = A_WIDTH // HALF

    @pl.when(c == 0)
    def _():
        xbuf[:, 0:HALO, :] = jnp.zeros((nb, HALO, xbuf.shape[2]), F32)
        state[...] = jnp.zeros(state.shape, F32)

    bdmask = _block_mask(HALF, HALF, A_DIM, A_DIM)
    bdmask_bf = _mask_bf(bdmask)
    t, s = _chunk_masks(n, HALF)
    incl = t >= s
    strict = t > s
    tri_bf = _mask_bf(_iota((n, n), 0) >= _iota((n, n), 1))

    seqs = []
    for i in range(nb):
        raw_main = main_ref[i]
        raw_small = small_ref[i]
        xbuf[i, HALO:HALO + n, 0:wm] = raw_main
        xbuf[i, HALO:HALO + n, wm:] = raw_small
        prev = xbuf[i, HALO - 1:HALO - 1 + n, :]
        xbuf[i, 0:HALO, :] = xbuf[i, n:n + HALO, :]
        x_main = raw_main + (prev[:, 0:wm] - raw_main) * mu_main_ref[...]
        x_small = raw_small + (prev[:, wm:] - raw_small) * mu_small_ref[...]
        r = x_main[:, 0:A_WIDTH]
        k = x_main[:, A_WIDTH:2 * A_WIDTH]
        v = x_main[:, 2 * A_WIDTH:3 * A_WIDTH]
        wlog = -_softplus(-(w0_ref[...] + _dot(_bf(jnp.tanh(x_small)), w2_ref[...]))) - 0.5
        logdec = -jnp.exp(wlog)
        a = _sigmoid(a0_ref[...] + _dot(_bf(x_small), a2_ref[...]))
        g = _dot(_bf(_sigmoid(x_small)), g2_ref[...])
        kkr = k * kk_ref[...]
        kk = kkr * lax.rsqrt(_seg_sum(kkr * kkr, bdmask_bf) + L2_EPS)
        k = k * (1.0 + (a - 1.0) * ka_ref[...])
        bonus = _seg_sum(r * k * rk_ref[...], bdmask_bf) * v
        cum = _exact_cols(tri_bf, logdec)
        e_neg = jnp.exp(-cum)
        e_rest = jnp.exp(cum[n - 1:n, :] - cum)
        b = kk * a
        seqs.append(dict(
            v=v, g=g, bonus=bonus,
            a_t=-kk * jnp.exp(cum - logdec),
            r_t=r * jnp.exp(cum),
            b_t=b * e_neg, k_t=k * e_neg,
            b_c=b * e_rest, k_c=k * e_rest,
            e_last=jnp.exp(cum[n - 1:n, :])))

    chains = [(i, hf) for i in range(nb) for hf in range(halves)]
    col = lambda name, i, hf: seqs[i][name][:, hf * HALF:(hf + 1) * HALF]
    s_old = [state[i, hf] for i, hf in chains]
    v_h = [col("v", i, hf) for i, hf in chains]
    bdv = [_bd(_bf(x), bdmask_bf, reps) for x in v_h]
    pr = [_dot_nt(_bf(jnp.concatenate([col("a_t", i, hf), col("r_t", i, hf)], axis=0)),
                  jnp.concatenate([_bd(_bf(col("b_t", i, hf)), bdmask_bf, reps),
                                   _bd(_bf(col("k_t", i, hf)), bdmask_bf, reps), _bf(s0)], axis=0))
          for (i, hf), s0 in zip(chains, s_old)]
    ab = [jnp.where(strict, x[0:n, 0:HALF], 0.0) for x in pr]
    rhs_u = [x[0:n, 2 * HALF:] + _dot(_bf(jnp.where(strict, x[0:n, HALF:2 * HALF], 0.0)), y)
             for x, y in zip(pr, bdv)]
    tinv = _tri_inv([-x for x in ab], t, s, bdmask_bf)
    u = [_dot(_bf(x), _bd(_bf(y), bdmask_bf, reps)) for x, y in zip(tinv, rhs_u)]
    o = [x[n:2 * n, 2 * HALF:] + _dot(
            _bf(jnp.concatenate([jnp.where(incl, x[n:2 * n, 0:HALF], 0.0),
                                 jnp.where(incl, x[n:2 * n, HALF:2 * HALF], 0.0)], axis=1)),
            jnp.concatenate([_bd(_bf(y), bdmask_bf, reps), z], axis=0))
         for x, y, z in zip(pr, u, bdv)]
    upd = [_dot_tn(_bf(jnp.concatenate([y, z], axis=0)),
                   _bf(jnp.concatenate([col("b_c", i, hf), col("k_c", i, hf)], axis=0)))
           for (i, hf), y, z in zip(chains, u, v_h)]
    for (i, hf), s0, x in zip(chains, s_old, upd):
        state[i, hf] = s0 * col("e_last", i, hf) + jnp.where(bdmask, x, 0.0)
    for i in range(nb):
        oi = jnp.concatenate(o[i * halves:(i + 1) * halves], axis=1)
        oc = oi - _seg_sum(oi, bdmask_bf) * (1.0 / A_DIM)
        var = _seg_sum(oc * oc, bdmask_bf) * (1.0 / A_DIM)
        on = oc * lax.rsqrt(var + A_GN_EPS)
        o_ref[i] = ((on * lnw_ref[...] + lnb_ref[...] + seqs[i]["bonus"]) * seqs[i]["g"]).astype(o_ref.dtype)


def _rwkv(p, params):
    bsz = p.shape[0]
    return _mixer_call(
        _rwkv_kernel, p, [(P_A_MAIN, 3 * A_WIDTH), (P_A_SMALL, 256)], params,
        [pltpu.VMEM((bsz, CHUNK + HALO, 3 * A_WIDTH + 256), F32),
         pltpu.VMEM((bsz, A_WIDTH // HALF, HALF, HALF), F32)], "rwkv7")


def _gla_kernel(main_ref, small_ref, gkw_ref, gkb_ref, nw_ref, o_ref, state):
    c = pl.program_id(0)
    nb = main_ref.shape[0]
    n = CHUNK

    @pl.when(c == 0)
    def _():
        state[...] = jnp.zeros(state.shape, F32)

    t, s = _chunk_masks(n, B_HEADS * n)
    incl = t >= s
    tri_bf = _mask_bf(_iota((n, n), 0) >= _iota((n, n), 1))
    kmask_bf = _mask_bf(_block_mask(B_HEADS * n, B_KEY_WIDTH, n, B_KEY))
    vmask_bf = _mask_bf(_block_mask(B_HEADS * n, B_WIDTH, n, B_VAL))
    smask = _block_mask(B_WIDTH, B_KEY_WIDTH, B_VAL, B_KEY)
    ones_bf = _mask_bf(_block_mask(HALF, HALF, B_VAL, B_VAL))

    qd, kd, k_c, e_last, v, g = [], [], [], [], [], []
    for i in range(nb):
        x = main_ref[i]
        q = x[:, 0:B_KEY_WIDTH] * (B_KEY ** -0.5)
        k = x[:, B_KEY_WIDTH:2 * B_KEY_WIDTH]
        v.append(x[:, 2 * B_KEY_WIDTH:2 * B_KEY_WIDTH + B_WIDTH])
        g.append(x[:, 2 * B_KEY_WIDTH + B_WIDTH:])
        gk = -_softplus(-(_dot(_bf(small_ref[i]), gkw_ref[...]) + gkb_ref[...])) / B_GATE_NORMALIZER
        bcum = _exact_cols(tri_bf, gk)
        qd.append(q * jnp.exp(bcum))
        kd.append(k * jnp.exp(-bcum))
        k_c.append(k * jnp.exp(bcum[n - 1:n, :] - bcum))
        e_last.append(jnp.exp(bcum[n - 1:n, :]))
    s_old = [state[i] for i in range(nb)]
    att = [jnp.where(incl, _dot_nt(_bf(x), _bd(_bf(y), kmask_bf, B_HEADS)), 0.0) for x, y in zip(qd, kd)]
    o = [_dot(_bf(x), _bd(_bf(y), vmask_bf, B_HEADS)) + _dot_nt(_bf(z), _bf(s0))
         for x, y, z, s0 in zip(att, v, qd, s_old)]
    upd = [_dot_tn(_bf(x), _bf(y)) for x, y in zip(v, k_c)]
    for i in range(nb):
        state[i] = s_old[i] * e_last[i] + jnp.where(smask, upd[i], 0.0)
        oi = o[i] * lax.rsqrt(_seg_sum(o[i] * o[i], ones_bf) * (1.0 / B_VAL) + NORM_EPS) * nw_ref[...]
        o_ref[i] = (oi * _silu(g[i])).astype(o_ref.dtype)


def _gla(p, params):
    bsz = p.shape[0]
    return _mixer_call(_gla_kernel, p, [(P_B_MAIN, 1536), (P_B_SMALL, SMALL)], params,
                       [pltpu.VMEM((bsz, B_WIDTH, B_KEY_WIDTH), F32)], "gla")


def _gdn_kernel(qkv_ref, z_ref, small_ref, cw_ref, alog_ref, dtb_ref, nw_ref, o_ref, xbuf, state):
    c = pl.program_id(0)
    nb = qkv_ref.shape[0]
    n = CHUNK
    hn = C_HEADS * n
    halves = C_WIDTH // HALF

    @pl.when(c == 0)
    def _():
        xbuf[:, 0:HALO, :] = jnp.zeros((nb, HALO, xbuf.shape[2]), F32)
        state[...] = jnp.zeros(state.shape, F32)

    smask = _block_mask(HALF, HALF, C_DIM, C_DIM)
    ones_bf = _mask_bf(smask)
    t, s = _chunk_masks(n, hn)
    incl = t >= s
    strict = t > s
    tri_bf = _mask_bf(_iota((n, n), 0) >= _iota((n, n), 1))
    bdmask_bf = _mask_bf(_block_mask(hn, hn, n, n))
    dmask_bf = _mask_bf(_block_mask(hn, C_WIDTH, n, C_DIM))
    lane = _iota((SMALL, C_WIDTH + hn), 1)
    head_of = jnp.where(lane < C_WIDTH, lane // C_DIM, (lane - C_WIDTH) // n)
    spread_g = _mask_bf(_iota((SMALL, C_WIDTH + hn), 0) == head_of)
    spread_b = _mask_bf(_iota((SMALL, C_WIDTH), 0) == C_HEADS + _iota((SMALL, C_WIDTH), 1) // C_DIM)

    seqs = []
    for i in range(nb):
        raw = qkv_ref[i]
        xbuf[i, HALO:HALO + n, :] = raw
        y = raw * cw_ref[C_CONV - 1:C_CONV, :]
        for j in range(1, C_CONV):
            y = y + xbuf[i, HALO - j:HALO - j + n, :] * cw_ref[C_CONV - 1 - j:C_CONV - j, :]
        xbuf[i, 0:HALO, :] = xbuf[i, n:n + HALO, :]
        qkv = _silu(y)
        q = qkv[:, 0:C_WIDTH]
        k = qkv[:, C_WIDTH:2 * C_WIDTH]
        v = qkv[:, 2 * C_WIDTH:]
        q = q * (lax.rsqrt(_seg_sum(q * q, ones_bf) + L2_EPS) * (C_DIM ** -0.5))
        k = k * lax.rsqrt(_seg_sum(k * k, ones_bf) + L2_EPS)
        small = small_ref[i]
        g_all = -jnp.exp(alog_ref[...]) * _softplus(small + dtb_ref[...])
        gam = _exact_rows(_exact_cols(tri_bf, g_all), spread_g)
        gam_w = gam[:, 0:C_WIDTH]
        gam_n = gam[:, C_WIDTH:]
        beta = _exact_rows(_sigmoid(small), spread_b)
        grow = jnp.sum(jnp.where(t == s, gam_n, 0.0), axis=0, keepdims=True)
        dec = jnp.exp(jnp.where(incl, gam_n - grow, 0.0))
        eg = jnp.exp(gam_w)
        glast = gam_w[n - 1:n, :]
        kb = k * beta
        seqs.append(dict(q=q, k=k, kb=kb, dec=dec, vb=v * beta, kbe=kb * eg, qg=q * eg,
                         kg=k * jnp.exp(glast - gam_w), e_last=jnp.exp(glast)))

    pr = [_dot_nt(_bf(jnp.concatenate([d["kb"], d["q"]], axis=0)), _bd(_bf(d["k"]), dmask_bf, C_HEADS))
          for d in seqs]
    m = [jnp.where(strict, x[0:n] * d["dec"], 0.0) for x, d in zip(pr, seqs)]
    qk = [jnp.where(incl, x[n:2 * n] * d["dec"], 0.0) for x, d in zip(pr, seqs)]
    tinv = _tri_inv(m, t, s, bdmask_bf)
    uw = [_dot(_bf(x), jnp.concatenate([_bd(_bf(d["vb"]), dmask_bf, C_HEADS),
                                        _bd(_bf(d["kbe"]), dmask_bf, C_HEADS)], axis=1))
          for x, d in zip(tinv, seqs)]
    chains = [(i, hf) for i in range(nb) for hf in range(halves)]
    s_old = [state[i, hf] for i, hf in chains]
    ws = [_dot(_bf(jnp.concatenate([uw[i][:, C_WIDTH + hf * HALF:C_WIDTH + (hf + 1) * HALF],
                                    seqs[i]["qg"][:, hf * HALF:(hf + 1) * HALF]], axis=0)), _bf(s0))
          for (i, hf), s0 in zip(chains, s_old)]
    vn = [uw[i][:, hf * HALF:(hf + 1) * HALF] - x[0:n] for (i, hf), x in zip(chains, ws)]
    upd = [_dot_tn(_bf(seqs[i]["kg"][:, hf * HALF:(hf + 1) * HALF]), _bf(x)) for (i, hf), x in zip(chains, vn)]
    for (i, hf), s0, x in zip(chains, s_old, upd):
        state[i, hf] = s0 * seqs[i]["e_last"][:, hf * HALF:(hf + 1) * HALF] + jnp.where(smask, x, 0.0)
    for i in range(nb):
        v_new = jnp.concatenate(vn[i * halves:(i + 1) * halves], axis=1)
        o_inter = jnp.concatenate([x[n:2 * n] for x in ws[i * halves:(i + 1) * halves]], axis=1)
        oi = o_inter + _dot(_bf(qk[i]), _bd(_bf(v_new), dmask_bf, C_HEADS))
        oi = oi * lax.rsqrt(_seg_sum(oi * oi, ones_bf) * (1.0 / C_DIM) + NORM_EPS) * nw_ref[...]
        o_ref[i] = (oi * _silu(z_ref[i])).astype(o_ref.dtype)


def _gdn(p, params):
    bsz = p.shape[0]
    return _mixer_call(
        _gdn_kernel, p, [(P_C_QKV, 3 * C_WIDTH), (P_C_Z, C_WIDTH), (P_C_SMALL, SMALL)], params,
        [pltpu.VMEM((bsz, CHUNK + HALO, 3 * C_WIDTH), F32),
         pltpu.VMEM((bsz, C_WIDTH // HALF, HALF, HALF), F32)], "gdn")


def _merge_kernel(h_ref, ya_ref, yb_ref, yc_ref, wg_ref, wb_ref, o_ref):
    h = h_ref[...]
    acc = None
    for i, y_ref in enumerate((ya_ref, yb_ref, yc_ref)):
        t = _sigmoid(_dot(h, wg_ref[i])) * _dot(y_ref[...], wb_ref[i])
        acc = t if acc is None else acc + t
    o_ref[...] = acc.astype(o_ref.dtype)


def _merge(hb, ya, yb, yc, wg, wb, layer, tm=1024, tn=512):
    m, d = hb.shape
    bw = ya.shape[1]
    tm = min(tm, m)
    yspec = pl.BlockSpec((tm, bw), lambda i, j: (i, 0))
    return pl.pallas_call(
        _merge_kernel,
        grid=(m // tm, d // tn),
        in_specs=[pl.BlockSpec((tm, d), lambda i, j: (i, 0)), yspec, yspec, yspec,
                  pl.BlockSpec((None, 3, d, tn), lambda i, j: (layer, 0, 0, j)),
                  pl.BlockSpec((None, 3, bw, tn), lambda i, j: (layer, 0, 0, j))],
        out_specs=pl.BlockSpec((tm, tn), lambda i, j: (i, j)),
        out_shape=jax.ShapeDtypeStruct((m, d), BF16),
        compiler_params=_cparams(("parallel", "parallel")),
        name="merge",
    )(hb, ya, yb, yc, wg, wb)


def _proj_ln_kernel(a_ref, w_ref, h_ref, g_ref, b_ref, o_ref, ob_ref):
    y = _layer_norm(DEEPNORM_ALPHA * h_ref[...] + _dot(a_ref[...], w_ref[...]), g_ref[...], b_ref[...])
    o_ref[...] = y
    ob_ref[...] = y.astype(BF16)


def _proj_ln(a, w, layer, h, g, b, tm, name):
    m, k = a.shape
    d = w.shape[2]
    tm = min(tm, m)
    return pl.pallas_call(
        _proj_ln_kernel,
        grid=(m // tm,),
        in_specs=[pl.BlockSpec((tm, k), lambda i: (i, 0)),
                  pl.BlockSpec((None, k, d), lambda i: (layer, 0, 0), pipeline_mode=pl.Buffered(1)),
                  pl.BlockSpec((tm, d), lambda i: (i, 0)),
                  pl.BlockSpec((1, d), lambda i: (0, 0)),
                  pl.BlockSpec((1, d), lambda i: (0, 0))],
        out_specs=[pl.BlockSpec((tm, d), lambda i: (i, 0)),
                   pl.BlockSpec((tm, d), lambda i: (i, 0))],
        out_shape=[jax.ShapeDtypeStruct((m, d), F32), jax.ShapeDtypeStruct((m, d), BF16)],
        compiler_params=_cparams(("parallel",)),
        name=name,
    )(a, w, h, g.reshape(1, d), b.reshape(1, d))


def _ffn_up_kernel(a_ref, wg_ref, wu_ref, cwg_ref, cwu_ref, bg_ref, bu_ref, o_ref, gcar, ucar, *, blocks_per_seq):
    i = pl.program_id(0)
    j = pl.program_id(1)
    tm = a_ref.shape[0]
    a = a_ref[...]
    first = (i % blocks_per_seq) == 0
    head_row = _iota((HALO, o_ref.shape[1]), 0)

    def shifted(u, tail, k):
        r = pltpu.roll(u, k, axis=0)
        head = jnp.where(head_row < k, pltpu.roll(tail, k, axis=0), r[0:HALO])
        return jnp.concatenate([head, r[HALO:]], axis=0)

    def conv(w_ref, cw_ref, b_ref, car):
        u = _dot(a, w_ref[...])
        tail = jnp.where(first, 0.0, car[j])
        car[j] = u[tm - HALO:tm, :]
        return (u * cw_ref[2:3, :] + shifted(u, tail, 1) * cw_ref[1:2, :]
                + shifted(u, tail, 2) * cw_ref[0:1, :] + b_ref[...])

    gate = conv(wg_ref, cwg_ref, bg_ref, gcar)
    up = conv(wu_ref, cwu_ref, bu_ref, ucar)
    o_ref[...] = (_silu(gate) * up).astype(o_ref.dtype)


def _ffn_up(hb, w_up, layer, conv_w, conv_b, seq, tm=1024, tn=512):
    m, d = hb.shape
    nf = w_up.shape[2] // 2
    nj = nf // tn
    tm = min(tm, seq)
    kern = functools.partial(_ffn_up_kernel, blocks_per_seq=seq // tm)
    return pl.pallas_call(
        kern,
        grid=(m // tm, nj),
        in_specs=[pl.BlockSpec((tm, d), lambda i, j: (i, 0)),
                  pl.BlockSpec((None, d, tn), lambda i, j: (layer, 0, j)),
                  pl.BlockSpec((None, d, tn), lambda i, j: (layer, 0, j + nj)),
                  pl.BlockSpec((FFN_CONV, tn), lambda i, j: (0, j)),
                  pl.BlockSpec((FFN_CONV, tn), lambda i, j: (0, j + nj)),
                  pl.BlockSpec((1, tn), lambda i, j: (0, j)),
                  pl.BlockSpec((1, tn), lambda i, j: (0, j + nj))],
        out_specs=pl.BlockSpec((tm, tn), lambda i, j: (i, j)),
        out_shape=jax.ShapeDtypeStruct((m, nf), BF16),
        scratch_shapes=[pltpu.VMEM((nj, HALO, tn), F32), pltpu.VMEM((nj, HALO, tn), F32)],
        compiler_params=_cparams(("arbitrary", "arbitrary")),
        name="ffn_up",
    )(hb, w_up, w_up, conv_w, conv_w, conv_b, conv_b)


def _pad_cols(t, width):
    return jnp.pad(t, ((0, 0), (0, width - t.shape[1])))


def _pad_rows(t, before, total):
    return jnp.pad(t, ((before, total - before - t.shape[0]), (0, 0)))


def _relayout_w_in(w):
    a_in = 3 * A_WIDTH + A_DECAY_LORA + A_ICLR_LORA + A_GATE_LORA
    b_in = 2 * B_KEY_WIDTH + B_WIDTH + B_GATE_LORA + B_WIDTH
    wa, wb, wc = w[..., :a_in], w[..., a_in:a_in + b_in], w[..., a_in + b_in:]
    b_qkv = wb[..., :2 * B_KEY_WIDTH + B_WIDTH]
    b_gk = wb[..., 2 * B_KEY_WIDTH + B_WIDTH:2 * B_KEY_WIDTH + B_WIDTH + B_GATE_LORA]
    b_g = wb[..., 2 * B_KEY_WIDTH + B_WIDTH + B_GATE_LORA:]
    c_qkv = wc[..., :3 * C_WIDTH]
    c_ab = wc[..., 3 * C_WIDTH:3 * C_WIDTH + 2 * C_HEADS]
    c_z = wc[..., 3 * C_WIDTH + 2 * C_HEADS:]

    def pad(t, width):
        return jnp.pad(t, [(0, 0)] * (t.ndim - 1) + [(0, width - t.shape[-1])])

    return jnp.concatenate([
        wa[..., :3 * A_WIDTH], b_qkv, b_g, c_qkv, c_z,
        pad(wa[..., 3 * A_WIDTH:], 256), pad(b_gk, SMALL), pad(c_ab, SMALL)], axis=-1)


def kernel(x, ln_in_g, ln_in_b, w_in, mu_a, a_w0, a_w2, a_a0, a_a2, a_g2, a_kk, a_ka, a_rk, a_lnx_w, a_lnx_b,
           b_gk_w2, b_gk_b, b_norm_w, c_conv_w, c_a_log, c_dt_bias, c_norm_w, w_gate, w_branch, w_out,
           ln1_g, ln1_b, w_up, ffn_conv_w, ffn_conv_b, w_down, ln2_g, ln2_b):
    bsz, seq, d = x.shape
    m = bsz * seq
    assert seq % CHUNK == 0
    h, hb = _ln_in(x.reshape(m, d), ln_in_g, ln_in_b)
    r1 = lambda t: t.reshape(1, -1)
    w_in_b = _relayout_w_in(w_in.astype(BF16))
    w_gate_b, w_branch_b, w_out_b = w_gate.astype(BF16), w_branch.astype(BF16), w_out.astype(BF16)
    w_up_b, w_down_b = w_up.astype(BF16), w_down.astype(BF16)
    for l in range(DEPTH):
        p = _matmul(hb, w_in_b, l, 512, P_WIDTH // 2, F32, "in_proj")
        p = p.reshape(bsz, seq, P_WIDTH)
        lo = A_DECAY_LORA
        ya = _rwkv(p, (
            r1(mu_a[l, :3 * A_WIDTH]), _pad_cols(r1(mu_a[l, 3 * A_WIDTH:]), 256),
            r1(a_w0[l]), _pad_rows(a_w2[l], 0, 256).astype(BF16),
            r1(a_a0[l]), _pad_rows(a_a2[l], lo, 256).astype(BF16),
            _pad_rows(a_g2[l], lo + A_ICLR_LORA, 256).astype(BF16),
            r1(a_kk[l]), r1(a_ka[l]), r1(a_rk[l]), r1(a_lnx_w[l]), r1(a_lnx_b[l])))
        yb = _gla(p, (_pad_rows(b_gk_w2[l], 0, SMALL).astype(BF16), r1(b_gk_b[l]),
                      jnp.tile(r1(b_norm_w[l]), (1, B_HEADS))))
        yc = _gdn(p, (c_conv_w[l], _pad_cols(r1(c_a_log[l]), SMALL), _pad_cols(r1(c_dt_bias[l]), SMALL),
                      jnp.tile(r1(c_norm_w[l]), (1, C_HEADS))))
        merged = _merge(hb, ya.reshape(m, -1), yb.reshape(m, -1), yc.reshape(m, -1), w_gate_b, w_branch_b, l)
        h, hb = _proj_ln(merged, w_out_b, l, h, ln1_g[l], ln1_b[l], 512, "out_proj")
        act = _ffn_up(hb, w_up_b, l, ffn_conv_w[l], r1(ffn_conv_b[l]), seq)
        h, hb = _proj_ln(act, w_down_b, l, h, ln2_g[l], ln2_b[l], 256, "ffn_down")
    return h.reshape(bsz, seq, d)
```

```python
import functools

import jax
import jax.numpy as jnp
from jax import lax
from jax.experimental import pallas as pl
from jax.experimental.pallas import tpu as pltpu

F32 = jnp.float32
BF16 = jnp.bfloat16

DEPTH = 2
A_HEADS, A_DIM, A_WIDTH = 8, 64, 512
A_DECAY_LORA, A_ICLR_LORA, A_GATE_LORA = 32, 32, 96
A_GN_EPS = 64e-5
B_HEADS, B_KEY, B_VAL = 4, 64, 128
B_KEY_WIDTH, B_WIDTH = 256, 512
B_GATE_LORA = 16
B_GATE_NORMALIZER = 16.0
C_HEADS, C_DIM, C_WIDTH = 4, 128, 512
C_CONV = 4
D_FF = 5632
FFN_CONV = 3
NORM_EPS = 1e-5
L2_EPS = 1e-6
DEEPNORM_ALPHA = (2 * DEPTH) ** 0.25

CHUNK = 64
HALO = 8
HALF = 256
SMALL = 128

P_A_MAIN = 0
P_B_MAIN = 1536
P_C_QKV = 3072
P_C_Z = 4608
P_A_SMALL = 5120
P_B_SMALL = 5376
P_C_SMALL = 5504
P_WIDTH = 5632

VMEM_LIMIT = 56 * 1024 * 1024


def _cparams(sem):
    return pltpu.CompilerParams(dimension_semantics=sem, vmem_limit_bytes=VMEM_LIMIT)


def _dot(a, b):
    return jnp.dot(a, b, preferred_element_type=F32)


def _dot_nt(a, b):
    return lax.dot_general(a, b, (((1,), (1,)), ((), ())), preferred_element_type=F32)


def _dot_tn(a, b):
    return lax.dot_general(a, b, (((0,), (0,)), ((), ())), preferred_element_type=F32)


def _bf(x):
    return x.astype(BF16)


def _layer_norm(x, g, b):
    mu = jnp.mean(x, -1, keepdims=True)
    xc = x - mu
    var = jnp.mean(xc * xc, -1, keepdims=True)
    return xc * lax.rsqrt(var + NORM_EPS) * g + b


def _softplus(x):
    return jnp.maximum(x, 0.0) + jnp.log1p(jnp.exp(-jnp.abs(x)))


def _sigmoid(x):
    return jax.nn.sigmoid(x)


def _silu(x):
    return x * jax.nn.sigmoid(x)


def _iota(shape, axis):
    return lax.broadcasted_iota(jnp.int32, shape, axis)


def _mask_bf(mask):
    return mask.astype(F32).astype(BF16)


def _block_mask(rows, cols, rblk, cblk):
    return (_iota((rows, cols), 0) // rblk) == (_iota((rows, cols), 1) // cblk)


def _bd(y, mask_bf, reps):
    return jnp.concatenate([y] * reps, axis=0) * mask_bf


def _pieces(x, count):
    out = []
    for _ in range(count - 1):
        p = _bf(x)
        out.append(p)
        x = x - p.astype(F32)
    out.append(_bf(x))
    return out


def _exact_rows(x, w_bf, count=3):
    n = x.shape[0]
    s = _dot(jnp.concatenate(_pieces(x, count), axis=0), w_bf)
    return sum(s[i * n:(i + 1) * n] for i in range(count))


def _exact_cols(w_bf, x, count=3):
    c = x.shape[1]
    s = _dot(w_bf, jnp.concatenate(_pieces(x, count), axis=1))
    return sum(s[:, i * c:(i + 1) * c] for i in range(count))


def _seg_sum(x, ones_bf):
    return jnp.concatenate([_exact_rows(x[:, i:i + HALF], ones_bf, 2) for i in range(0, x.shape[1], HALF)],
                           axis=1)


def _chunk_masks(n, width):
    return _iota((n, width), 0), _iota((n, width), 1) % n


def _tri_inv(ms, t, s, bdmask_bf):
    reps = ms[0].shape[1] // ms[0].shape[0]

    def mm(a_list, b_list):
        return [_dot(_bf(a), _bd(_bf(b), bdmask_bf, reps)) for a, b in zip(a_list, b_list)]

    def add(a_list, b_list):
        return [a + b for a, b in zip(a_list, b_list)]

    def sub(a_list, b_list):
        return [a - b for a, b in zip(a_list, b_list)]

    n = ms[0].shape[0]
    eye = (t == s).astype(F32)
    same16 = (t // 16) == (s // 16)
    same32 = (t // 32) == (s // 32)
    m16 = [jnp.where(same16, m, 0.0) for m in ms]
    x = [eye - m for m in m16]
    p = mm(m16, m16)
    for _ in range(2):
        xp = mm([jnp.concatenate([a, b], axis=0) for a, b in zip(x, p)], p)
        x = [a + b[0:n] for a, b in zip(x, xp)]
        p = [b[n:2 * n] for b in xp]
    x = add(x, mm(x, p))
    o32 = [jnp.where(same32 & jnp.logical_not(same16), m, 0.0) for m in ms]
    x = sub(x, mm(x, mm(o32, x)))
    o64 = [jnp.where(same32, 0.0, m) for m in ms]
    x = sub(x, mm(x, mm(o64, x)))
    return x


def _ln_kernel(x_ref, g_ref, b_ref, o_ref, ob_ref):
    y = _layer_norm(x_ref[...], g_ref[...], b_ref[...])
    o_ref[...] = y
    ob_ref[...] = y.astype(BF16)


def _ln_in(x, g, b, tm=256):
    m, d = x.shape
    return pl.pallas_call(
        _ln_kernel,
        grid=(m // tm,),
        in_specs=[pl.BlockSpec((tm, d), lambda i: (i, 0)),
                  pl.BlockSpec((1, d), lambda i: (0, 0)),
                  pl.BlockSpec((1, d), lambda i: (0, 0))],
        out_specs=[pl.BlockSpec((tm, d), lambda i: (i, 0)),
                   pl.BlockSpec((tm, d), lambda i: (i, 0))],
        out_shape=[jax.ShapeDtypeStruct((m, d), F32), jax.ShapeDtypeStruct((m, d), BF16)],
        compiler_params=_cparams(("parallel",)),
        name="ln_in",
    )(x, g.reshape(1, d), b.reshape(1, d))


def _mm_kernel(a_ref, w_ref, o_ref):
    o_ref[...] = _dot(a_ref[...], w_ref[...]).astype(o_ref.dtype)


def _matmul(a, w, layer, tm, tn, out_dtype, name):
    m, k = a.shape
    n = w.shape[2]
    tm = min(tm, m)
    return pl.pallas_call(
        _mm_kernel,
        grid=(n // tn, m // tm),
        in_specs=[pl.BlockSpec((tm, k), lambda j, i: (i, 0)),
                  pl.BlockSpec((None, k, tn), lambda j, i: (layer, 0, j))],
        out_specs=pl.BlockSpec((tm, tn), lambda j, i: (i, j)),
        out_shape=jax.ShapeDtypeStruct((m, n), out_dtype),
        compiler_params=_cparams(("parallel", "parallel")),
        name=name,
    )(a, w)


def _mixer_call(kern, p, blocks, params, scratch, name):
    bsz, seq, _ = p.shape
    n = CHUNK
    full = lambda arr: pl.BlockSpec(arr.shape, lambda c: (0,) * arr.ndim)
    in_specs = [pl.BlockSpec((bsz, n, width), functools.partial(lambda c, blk: (0, c, blk), blk=col // width))
                for col, width in blocks]
    return pl.pallas_call(
        kern,
        grid=(seq // n,),
        in_specs=in_specs + [full(t) for t in params],
        out_specs=pl.BlockSpec((bsz, n, 512), lambda c: (0, c, 0)),
        out_shape=jax.ShapeDtypeStruct((bsz, seq, 512), BF16),
        scratch_shapes=scratch,
        compiler_params=_cparams(("arbitrary",)),
        name=name,
    )(*([p] * len(blocks)), *params)


def _rwkv_kernel(main_ref, small_ref, mu_main_ref, mu_small_ref, w0_ref, w2_ref, a0_ref, a2_ref, g2_ref,
                 kk_ref, ka_ref, rk_ref, lnw_ref, lnb_ref, o_ref, xbuf, state):
    c = pl.program_id(0)
    nb = main_ref.shape[0]
    n = CHUNK
    wm = 3 * A_WIDTH
    reps = HALF // A_DIM
    halves = A_WIDTH // HALF

    @pl.when(c == 0)
    def _():
        xbuf[:, 0:HALO, :] = jnp.zeros((nb, HALO, xbuf.shape[2]), F32)
        state[...] = jnp.zeros(state.shape, F32)

    bdmask = _block_mask(HALF, HALF, A_DIM, A_DIM)
    bdmask_bf = _mask_bf(bdmask)
    t, s = _chunk_masks(n, HALF)
    incl = t >= s
    strict = t > s

    x_main, x_small = [], []
    for i in range(nb):
        raw_main = main_ref[i]
        raw_small = small_ref[i]
        xbuf[i, HALO:HALO + n, 0:wm] = raw_main
        xbuf[i, HALO:HALO + n, wm:] = raw_small
        prev = xbuf[i, HALO - 1:HALO - 1 + n, :]
        xbuf[i, 0:HALO, :] = xbuf[i, n:n + HALO, :]
        x_main.append(raw_main + (prev[:, 0:wm] - raw_main) * mu_main_ref[...])
        x_small.append(raw_small + (prev[:, wm:] - raw_small) * mu_small_ref[...])
    x_main = jnp.concatenate(x_main, axis=0)
    x_small = jnp.concatenate(x_small, axis=0)
    r = x_main[:, 0:A_WIDTH]
    k = x_main[:, A_WIDTH:2 * A_WIDTH]
    v_all = x_main[:, 2 * A_WIDTH:3 * A_WIDTH]
    wlog = -_softplus(-(w0_ref[...] + _dot(_bf(jnp.tanh(x_small)), w2_ref[...]))) - 0.5
    logdec = -jnp.exp(wlog)
    a = _sigmoid(a0_ref[...] + _dot(_bf(x_small), a2_ref[...]))
    g_all = _dot(_bf(_sigmoid(x_small)), g2_ref[...])
    kkr = k * kk_ref[...]
    k = k * (1.0 + (a - 1.0) * ka_ref[...])
    sums = _seg_sum(jnp.concatenate([kkr * kkr, r * k * rk_ref[...]], axis=0), bdmask_bf)
    kk = kkr * lax.rsqrt(sums[0:nb * n] + L2_EPS)
    bonus_all = sums[nb * n:] * v_all
    seq_tri_bf = _mask_bf((_iota((nb * n, nb * n), 0) >= _iota((nb * n, nb * n), 1))
                          & _block_mask(nb * n, nb * n, n, n))
    cum_all = _exact_cols(seq_tri_bf, logdec)
    b_all = kk * a
    a_t_all = -kk * jnp.exp(cum_all - logdec)
    r_t_all = r * jnp.exp(cum_all)
    e_neg = jnp.exp(-cum_all)
    b_t_all = b_all * e_neg
    k_t_all = k * e_neg
    seqs = []
    for i in range(nb):
        rows = slice(i * n, (i + 1) * n)
        cum = cum_all[rows]
        e_rest = jnp.exp(cum[n - 1:n, :] - cum)
        seqs.append(dict(
            v=v_all[rows], g=g_all[rows], bonus=bonus_all[rows],
            a_t=a_t_all[rows], r_t=r_t_all[rows], b_t=b_t_all[rows], k_t=k_t_all[rows],
            b_c=b_all[rows] * e_rest, k_c=k[rows] * e_rest,
            e_last=jnp.exp(cum[n - 1:n, :])))

    chains = [(i, hf) for i in range(nb) for hf in range(halves)]
    col = lambda name, i, hf: seqs[i][name][:, hf * HALF:(hf + 1) * HALF]
    s_old = [state[i, hf] for i, hf in chains]
    v_h = [col("v", i, hf) for i, hf in chains]
    bdv = [_bd(_bf(x), bdmask_bf, reps) for x in v_h]
    pr = [_dot_nt(_bf(jnp.concatenate([col("a_t", i, hf), col("r_t", i, hf)], axis=0)),
                  jnp.concatenate([_bd(_bf(col("b_t", i, hf)), bdmask_bf, reps),
                                   _bd(_bf(col("k_t", i, hf)), bdmask_bf, reps), _bf(s0)], axis=0))
          for (i, hf), s0 in zip(chains, s_old)]
    ab = [jnp.where(strict, x[0:n, 0:HALF], 0.0) for x in pr]
    rhs_u = [x[0:n, 2 * HALF:] + _dot(_bf(jnp.where(strict, x[0:n, HALF:2 * HALF], 0.0)), y)
             for x, y in zip(pr, bdv)]
    tinv = _tri_inv([-x for x in ab], t, s, bdmask_bf)
    u = [_dot(_bf(x), _bd(_bf(y), bdmask_bf, reps)) for x, y in zip(tinv, rhs_u)]
    o = [x[n:2 * n, 2 * HALF:] + _dot(
            _bf(jnp.concatenate([jnp.where(incl, x[n:2 * n, 0:HALF], 0.0),
                                 jnp.where(incl, x[n:2 * n, HALF:2 * HALF], 0.0)], axis=1)),
            jnp.concatenate([_bd(_bf(y), bdmask_bf, reps), z], axis=0))
         for x, y, z in zip(pr, u, bdv)]
    upd = [_dot_tn(_bf(jnp.concatenate([y, z], axis=0)),
                   _bf(jnp.concatenate([col("b_c", i, hf), col("k_c", i, hf)], axis=0)))
           for (i, hf), y, z in zip(chains, u, v_h)]
    for (i, hf), s0, x in zip(chains, s_old, upd):
        state[i, hf] = s0 * col("e_last", i, hf) + jnp.where(bdmask, x, 0.0)
    o_all = jnp.concatenate([jnp.concatenate(o[i * halves:(i + 1) * halves], axis=1) for i in range(nb)], axis=0)
    oc = o_all - _seg_sum(o_all, bdmask_bf) * (1.0 / A_DIM)
    var = _seg_sum(oc * oc, bdmask_bf) * (1.0 / A_DIM)
    out = ((oc * lax.rsqrt(var + A_GN_EPS)) * lnw_ref[...] + lnb_ref[...] + bonus_all) * g_all
    for i in range(nb):
        o_ref[i] = out[i * n:(i + 1) * n].astype(o_ref.dtype)


def _rwkv(p, params):
    bsz = p.shape[0]
    return _mixer_call(
        _rwkv_kernel, p, [(P_A_MAIN, 3 * A_WIDTH), (P_A_SMALL, 256)], params,
        [pltpu.VMEM((bsz, CHUNK + HALO, 3 * A_WIDTH + 256), F32),
         pltpu.VMEM((bsz, A_WIDTH // HALF, HALF, HALF), F32)], "rwkv7")


def _gla_kernel(main_ref, small_ref, gkw_ref, gkb_ref, nw_ref, o_ref, state):
    c = pl.program_id(0)
    nb = main_ref.shape[0]
    n = CHUNK

    @pl.when(c == 0)
    def _():
        state[...] = jnp.zeros(state.shape, F32)

    t, s = _chunk_masks(n, B_HEADS * n)
    incl = t >= s
    tri_bf = _mask_bf(_iota((n, n), 0) >= _iota((n, n), 1))
    kmask_bf = _mask_bf(_block_mask(B_HEADS * n, B_KEY_WIDTH, n, B_KEY))
    vmask_bf = _mask_bf(_block_mask(B_HEADS * n, B_WIDTH, n, B_VAL))
    smask = _block_mask(B_WIDTH, B_KEY_WIDTH, B_VAL, B_KEY)
    ones_bf = _mask_bf(_block_mask(HALF, HALF, B_VAL, B_VAL))

    qd, kd, k_c, e_last, v, g = [], [], [], [], [], []
    for i in range(nb):
        x = main_ref[i]
        q = x[:, 0:B_KEY_WIDTH] * (B_KEY ** -0.5)
        k = x[:, B_KEY_WIDTH:2 * B_KEY_WIDTH]
        v.append(x[:, 2 * B_KEY_WIDTH:2 * B_KEY_WIDTH + B_WIDTH])
        g.append(x[:, 2 * B_KEY_WIDTH + B_WIDTH:])
        gk = -_softplus(-(_dot(_bf(small_ref[i]), gkw_ref[...]) + gkb_ref[...])) / B_GATE_NORMALIZER
        bcum = _exact_cols(tri_bf, gk)
        qd.append(q * jnp.exp(bcum))
        kd.append(k * jnp.exp(-bcum))
        k_c.append(k * jnp.exp(bcum[n - 1:n, :] - bcum))
        e_last.append(jnp.exp(bcum[n - 1:n, :]))
    s_old = [state[i] for i in range(nb)]
    att = [jnp.where(incl, _dot_nt(_bf(x), _bd(_bf(y), kmask_bf, B_HEADS)), 0.0) for x, y in zip(qd, kd)]
    o = [_dot(_bf(x), _bd(_bf(y), vmask_bf, B_HEADS)) + _dot_nt(_bf(z), _bf(s0))
         for x, y, z, s0 in zip(att, v, qd, s_old)]
    upd = [_dot_tn(_bf(x), _bf(y)) for x, y in zip(v, k_c)]
    for i in range(nb):
        state[i] = s_old[i] * e_last[i] + jnp.where(smask, upd[i], 0.0)
        oi = o[i] * lax.rsqrt(_seg_sum(o[i] * o[i], ones_bf) * (1.0 / B_VAL) + NORM_EPS) * nw_ref[...]
        o_ref[i] = (oi * _silu(g[i])).astype(o_ref.dtype)


def _gla(p, params):
    bsz = p.shape[0]
    return _mixer_call(_gla_kernel, p, [(P_B_MAIN, 1536), (P_B_SMALL, SMALL)], params,
                       [pltpu.VMEM((bsz, B_WIDTH, B_KEY_WIDTH), F32)], "gla")


def _gdn_kernel(qkv_ref, z_ref, small_ref, cw_ref, alog_ref, dtb_ref, nw_ref, o_ref, xbuf, state):
    c = pl.program_id(0)
    nb = qkv_ref.shape[0]
    n = CHUNK
    hn = C_HEADS * n
    halves = C_WIDTH // HALF

    @pl.when(c == 0)
    def _():
        xbuf[:, 0:HALO, :] = jnp.zeros((nb, HALO, xbuf.shape[2]), F32)
        state[...] = jnp.zeros(state.shape, F32)

    smask = _block_mask(HALF, HALF, C_DIM, C_DIM)
    ones_bf = _mask_bf(smask)
    t, s = _chunk_masks(n, hn)
    incl = t >= s
    strict = t > s
    bdmask_bf = _mask_bf(_block_mask(hn, hn, n, n))
    dmask_bf = _mask_bf(_block_mask(hn, C_WIDTH, n, C_DIM))
    lane = _iota((SMALL, C_WIDTH + hn), 1)
    head_of = jnp.where(lane < C_WIDTH, lane // C_DIM, (lane - C_WIDTH) // n)
    spread_g = _mask_bf(_iota((SMALL, C_WIDTH + hn), 0) == head_of)
    spread_b = _mask_bf(_iota((SMALL, C_WIDTH), 0) == C_HEADS + _iota((SMALL, C_WIDTH), 1) // C_DIM)

    seqs = []
    qkv = []
    for i in range(nb):
        raw = qkv_ref[i]
        xbuf[i, HALO:HALO + n, :] = raw
        y = raw * cw_ref[C_CONV - 1:C_CONV, :]
        for j in range(1, C_CONV):
            y = y + xbuf[i, HALO - j:HALO - j + n, :] * cw_ref[C_CONV - 1 - j:C_CONV - j, :]
        xbuf[i, 0:HALO, :] = xbuf[i, n:n + HALO, :]
        qkv.append(_silu(y))
    qkv = jnp.concatenate(qkv, axis=0)
    q = qkv[:, 0:C_WIDTH]
    k = qkv[:, C_WIDTH:2 * C_WIDTH]
    v = qkv[:, 2 * C_WIDTH:]
    sums = _seg_sum(jnp.concatenate([q * q, k * k], axis=0), ones_bf)
    q = q * (lax.rsqrt(sums[0:nb * n] + L2_EPS) * (C_DIM ** -0.5))
    k = k * lax.rsqrt(sums[nb * n:] + L2_EPS)
    small = jnp.concatenate([small_ref[i] for i in range(nb)], axis=0)
    g_all = -jnp.exp(alog_ref[...]) * _softplus(small + dtb_ref[...])
    seq_tri_bf = _mask_bf((_iota((nb * n, nb * n), 0) >= _iota((nb * n, nb * n), 1))
                          & _block_mask(nb * n, nb * n, n, n))
    gam = _exact_rows(_exact_cols(seq_tri_bf, g_all), spread_g)
    gam_w_all = gam[:, 0:C_WIDTH]
    gam_n_all = gam[:, C_WIDTH:]
    beta = _exact_rows(_sigmoid(small), spread_b)
    eg = jnp.exp(gam_w_all)
    kb = k * beta
    vb_all, kbe_all, qg_all = v * beta, kb * eg, q * eg
    for i in range(nb):
        rows = slice(i * n, (i + 1) * n)
        gam_w, gam_n = gam_w_all[rows], gam_n_all[rows]
        grow = jnp.sum(jnp.where(t == s, gam_n, 0.0), axis=0, keepdims=True)
        dec = jnp.exp(jnp.where(incl, gam_n - grow, 0.0))
        glast = gam_w[n - 1:n, :]
        seqs.append(dict(q=q[rows], k=k[rows], kb=kb[rows], dec=dec, vb=vb_all[rows], kbe=kbe_all[rows],
                         qg=qg_all[rows], kg=k[rows] * jnp.exp(glast - gam_w), e_last=jnp.exp(glast)))

    pr = [_dot_nt(_bf(jnp.concatenate([d["kb"], d["q"]], axis=0)), _bd(_bf(d["k"]), dmask_bf, C_HEADS))
          for d in seqs]
    m = [jnp.where(strict, x[0:n] * d["dec"], 0.0) for x, d in zip(pr, seqs)]
    qk = [jnp.where(incl, x[n:2 * n] * d["dec"], 0.0) for x, d in zip(pr, seqs)]
    tinv = _tri_inv(m, t, s, bdmask_bf)
    uw = [_dot(_bf(x), jnp.concatenate([_bd(_bf(d["vb"]), dmask_bf, C_HEADS),
                                        _bd(_bf(d["kbe"]), dmask_bf, C_HEADS)], axis=1))
          for x, d in zip(tinv, seqs)]
    chains = [(i, hf) for i in range(nb) for hf in range(halves)]
    s_old = [state[i, hf] for i, hf in chains]
    ws = [_dot(_bf(jnp.concatenate([uw[i][:, C_WIDTH + hf * HALF:C_WIDTH + (hf + 1) * HALF],
                                    seqs[i]["qg"][:, hf * HALF:(hf + 1) * HALF]], axis=0)), _bf(s0))
          for (i, hf), s0 in zip(chains, s_old)]
    vn = [uw[i][:, hf * HALF:(hf + 1) * HALF] - x[0:n] for (i, hf), x in zip(chains, ws)]
    upd = [_dot_tn(_bf(seqs[i]["kg"][:, hf * HALF:(hf + 1) * HALF]), _bf(x)) for (i, hf), x in zip(chains, vn)]
    for (i, hf), s0, x in zip(chains, s_old, upd):
        state[i, hf] = s0 * seqs[i]["e_last"][:, hf * HALF:(hf + 1) * HALF] + jnp.where(smask, x, 0.0)
    outs = []
    for i in range(nb):
        v_new = jnp.concatenate(vn[i * halves:(i + 1) * halves], axis=1)
        o_inter = jnp.concatenate([x[n:2 * n] for x in ws[i * halves:(i + 1) * halves]], axis=1)
        outs.append(o_inter + _dot(_bf(qk[i]), _bd(_bf(v_new), dmask_bf, C_HEADS)))
    o_all = jnp.concatenate(outs, axis=0)
    o_all = o_all * lax.rsqrt(_seg_sum(o_all * o_all, ones_bf) * (1.0 / C_DIM) + NORM_EPS) * nw_ref[...]
    for i in range(nb):
        o_ref[i] = (o_all[i * n:(i + 1) * n] * _silu(z_ref[i])).astype(o_ref.dtype)


def _gdn(p, params):
    bsz = p.shape[0]
    return _mixer_call(
        _gdn_kernel, p, [(P_C_QKV, 3 * C_WIDTH), (P_C_Z, C_WIDTH), (P_C_SMALL, SMALL)], params,
        [pltpu.VMEM((bsz, CHUNK + HALO, 3 * C_WIDTH), F32),
         pltpu.VMEM((bsz, C_WIDTH // HALF, HALF, HALF), F32)], "gdn")


def _merge_kernel(h_ref, ya_ref, yb_ref, yc_ref, wg_ref, wb_ref, o_ref):
    h = h_ref[...]
    acc = None
    for i, y_ref in enumerate((ya_ref, yb_ref, yc_ref)):
        t = _sigmoid(_dot(h, wg_ref[i])) * _dot(y_ref[...], wb_ref[i])
        acc = t if acc is None else acc + t
    o_ref[...] = acc.astype(o_ref.dtype)


def _merge(hb, ya, yb, yc, wg, wb, layer, tm=1024, tn=512):
    m, d = hb.shape
    bw = ya.shape[1]
    tm = min(tm, m)
    yspec = pl.BlockSpec((tm, bw), lambda i, j: (i, 0))
    return pl.pallas_call(
        _merge_kernel,
        grid=(m // tm, d // tn),
        in_specs=[pl.BlockSpec((tm, d), lambda i, j: (i, 0)), yspec, yspec, yspec,
                  pl.BlockSpec((None, 3, d, tn), lambda i, j: (layer, 0, 0, j)),
                  pl.BlockSpec((None, 3, bw, tn), lambda i, j: (layer, 0, 0, j))],
        out_specs=pl.BlockSpec((tm, tn), lambda i, j: (i, j)),
        out_shape=jax.ShapeDtypeStruct((m, d), BF16),
        compiler_params=_cparams(("parallel", "parallel")),
        name="merge",
    )(hb, ya, yb, yc, wg, wb)


def _proj_ln_kernel(a_ref, w_ref, h_ref, g_ref, b_ref, o_ref, ob_ref):
    y = _layer_norm(DEEPNORM_ALPHA * h_ref[...] + _dot(a_ref[...], w_ref[...]), g_ref[...], b_ref[...])
    o_ref[...] = y
    ob_ref[...] = y.astype(BF16)


def _proj_ln(a, w, layer, h, g, b, tm, name):
    m, k = a.shape
    d = w.shape[2]
    tm = min(tm, m)
    return pl.pallas_call(
        _proj_ln_kernel,
        grid=(m // tm,),
        in_specs=[pl.BlockSpec((tm, k), lambda i: (i, 0)),
                  pl.BlockSpec((None, k, d), lambda i: (layer, 0, 0), pipeline_mode=pl.Buffered(1)),
                  pl.BlockSpec((tm, d), lambda i: (i, 0)),
                  pl.BlockSpec((1, d), lambda i: (0, 0)),
                  pl.BlockSpec((1, d), lambda i: (0, 0))],
        out_specs=[pl.BlockSpec((tm, d), lambda i: (i, 0)),
                   pl.BlockSpec((tm, d), lambda i: (i, 0))],
        out_shape=[jax.ShapeDtypeStruct((m, d), F32), jax.ShapeDtypeStruct((m, d), BF16)],
        compiler_params=_cparams(("parallel",)),
        name=name,
    )(a, w, h, g.reshape(1, d), b.reshape(1, d))


def _ffn_up_kernel(a_ref, wg_ref, wu_ref, cwg_ref, cwu_ref, bg_ref, bu_ref, o_ref, gcar, ucar, *, blocks_per_seq):
    i = pl.program_id(0)
    j = pl.program_id(1)
    tm = a_ref.shape[0]
    a = a_ref[...]
    first = (i % blocks_per_seq) == 0
    head_row = _iota((HALO, o_ref.shape[1]), 0)

    def shifted(u, tail, k):
        r = pltpu.roll(u, k, axis=0)
        head = jnp.where(head_row < k, pltpu.roll(tail, k, axis=0), r[0:HALO])
        return jnp.concatenate([head, r[HALO:]], axis=0)

    def conv(w_ref, cw_ref, b_ref, car):
        u = _dot(a, w_ref[...])
        tail = jnp.where(first, 0.0, car[j])
        car[j] = u[tm - HALO:tm, :]
        return (u * cw_ref[2:3, :] + shifted(u, tail, 1) * cw_ref[1:2, :]
                + shifted(u, tail, 2) * cw_ref[0:1, :] + b_ref[...])

    gate = conv(wg_ref, cwg_ref, bg_ref, gcar)
    up = conv(wu_ref, cwu_ref, bu_ref, ucar)
    o_ref[...] = (_silu(gate) * up).astype(o_ref.dtype)


def _ffn_up(hb, w_up, layer, conv_w, conv_b, seq, tm=1024, tn=512):
    m, d = hb.shape
    nf = w_up.shape[2] // 2
    nj = nf // tn
    tm = min(tm, seq)
    kern = functools.partial(_ffn_up_kernel, blocks_per_seq=seq // tm)
    return pl.pallas_call(
        kern,
        grid=(m // tm, nj),
        in_specs=[pl.BlockSpec((tm, d), lambda i, j: (i, 0)),
                  pl.BlockSpec((None, d, tn), lambda i, j: (layer, 0, j)),
                  pl.BlockSpec((None, d, tn), lambda i, j: (layer, 0, j + nj)),
                  pl.BlockSpec((FFN_CONV, tn), lambda i, j: (0, j)),
                  pl.BlockSpec((FFN_CONV, tn), lambda i, j: (0, j + nj)),
                  pl.BlockSpec((1, tn), lambda i, j: (0, j)),
                  pl.BlockSpec((1, tn), lambda i, j: (0, j + nj))],
        out_specs=pl.BlockSpec((tm, tn), lambda i, j: (i, j)),
        out_shape=jax.ShapeDtypeStruct((m, nf), BF16),
        scratch_shapes=[pltpu.VMEM((nj, HALO, tn), F32), pltpu.VMEM((nj, HALO, tn), F32)],
        compiler_params=_cparams(("arbitrary", "arbitrary")),
        name="ffn_up",
    )(hb, w_up, w_up, conv_w, conv_w, conv_b, conv_b)


def _pad_cols(t, width):
    return jnp.pad(t, ((0, 0), (0, width - t.shape[1])))


def _pad_rows(t, before, total):
    return jnp.pad(t, ((before, total - before - t.shape[0]), (0, 0)))


def _relayout_w_in(w):
    a_in = 3 * A_WIDTH + A_DECAY_LORA + A_ICLR_LORA + A_GATE_LORA
    b_in = 2 * B_KEY_WIDTH + B_WIDTH + B_GATE_LORA + B_WIDTH
    wa, wb, wc = w[..., :a_in], w[..., a_in:a_in + b_in], w[..., a_in + b_in:]
    b_qkv = wb[..., :2 * B_KEY_WIDTH + B_WIDTH]
    b_gk = wb[..., 2 * B_KEY_WIDTH + B_WIDTH:2 * B_KEY_WIDTH + B_WIDTH + B_GATE_LORA]
    b_g = wb[..., 2 * B_KEY_WIDTH + B_WIDTH + B_GATE_LORA:]
    c_qkv = wc[..., :3 * C_WIDTH]
    c_ab = wc[..., 3 * C_WIDTH:3 * C_WIDTH + 2 * C_HEADS]
    c_z = wc[..., 3 * C_WIDTH + 2 * C_HEADS:]

    def pad(t, width):
        return jnp.pad(t, [(0, 0)] * (t.ndim - 1) + [(0, width - t.shape[-1])])

    return jnp.concatenate([
        wa[..., :3 * A_WIDTH], b_qkv, b_g, c_qkv, c_z,
        pad(wa[..., 3 * A_WIDTH:], 256), pad(b_gk, SMALL), pad(c_ab, SMALL)], axis=-1)


def kernel(x, ln_in_g, ln_in_b, w_in, mu_a, a_w0, a_w2, a_a0, a_a2, a_g2, a_kk, a_ka, a_rk, a_lnx_w, a_lnx_b,
           b_gk_w2, b_gk_b, b_norm_w, c_conv_w, c_a_log, c_dt_bias, c_norm_w, w_gate, w_branch, w_out,
           ln1_g, ln1_b, w_up, ffn_conv_w, ffn_conv_b, w_down, ln2_g, ln2_b):
    bsz, seq, d = x.shape
    m = bsz * seq
    assert seq % CHUNK == 0
    h, hb = _ln_in(x.reshape(m, d), ln_in_g, ln_in_b)
    r1 = lambda t: t.reshape(1, -1)
    w_in_b = _relayout_w_in(w_in.astype(BF16))
    w_gate_b, w_branch_b, w_out_b = w_gate.astype(BF16), w_branch.astype(BF16), w_out.astype(BF16)
    w_up_b, w_down_b = w_up.astype(BF16), w_down.astype(BF16)
    for l in range(DEPTH):
        p = _matmul(hb, w_in_b, l, 512, P_WIDTH // 2, F32, "in_proj")
        p = p.reshape(bsz, seq, P_WIDTH)
        lo = A_DECAY_LORA
        ya = _rwkv(p, (
            r1(mu_a[l, :3 * A_WIDTH]), _pad_cols(r1(mu_a[l, 3 * A_WIDTH:]), 256),
            r1(a_w0[l]), _pad_rows(a_w2[l], 0, 256).astype(BF16),
            r1(a_a0[l]), _pad_rows(a_a2[l], lo, 256).astype(BF16),
            _pad_rows(a_g2[l], lo + A_ICLR_LORA, 256).astype(BF16),
            r1(a_kk[l]), r1(a_ka[l]), r1(a_rk[l]), r1(a_lnx_w[l]), r1(a_lnx_b[l])))
        yb = _gla(p, (_pad_rows(b_gk_w2[l], 0, SMALL).astype(BF16), r1(b_gk_b[l]),
                      jnp.tile(r1(b_norm_w[l]), (1, B_HEADS))))
        yc = _gdn(p, (c_conv_w[l], _pad_cols(r1(c_a_log[l]), SMALL), _pad_cols(r1(c_dt_bias[l]), SMALL),
                      jnp.tile(r1(c_norm_w[l]), (1, C_HEADS))))
        merged = _merge(hb, ya.reshape(m, -1), yb.reshape(m, -1), yc.reshape(m, -1), w_gate_b, w_branch_b, l)
        h, hb = _proj_ln(merged, w_out_b, l, h, ln1_g[l], ln1_b[l], 512, "out_proj")
        act = _ffn_up(hb, w_up_b, l, ffn_conv_w[l], r1(ffn_conv_b[l]), seq)
        h, hb = _proj_ln(act, w_down_b, l, h, ln2_g[l], ln2_b[l], 256, "ffn_down")
    return h.reshape(bsz, seq, d)
```

```python
import functools

import jax
import jax.numpy as jnp
from jax import lax
from jax.experimental import pallas as pl
from jax.experimental.pallas import tpu as pltpu

F32 = jnp.float32
BF16 = jnp.bfloat16

DEPTH = 2
A_HEADS, A_DIM, A_WIDTH = 8, 64, 512
A_DECAY_LORA, A_ICLR_LORA, A_GATE_LORA = 32, 32, 96
A_GN_EPS = 64e-5
B_HEADS, B_KEY, B_VAL = 4, 64, 128
B_KEY_WIDTH, B_WIDTH = 256, 512
B_GATE_LORA = 16
B_GATE_NORMALIZER = 16.0
C_HEADS, C_DIM, C_WIDTH = 4, 128, 512
C_CONV = 4
D_FF = 5632
FFN_CONV = 3
NORM_EPS = 1e-5
L2_EPS = 1e-6
DEEPNORM_ALPHA = (2 * DEPTH) ** 0.25

CHUNK = 64
HALO = 8
HALF = 256
SMALL = 128

P_A_MAIN = 0
P_B_MAIN = 1536
P_C_QKV = 3072
P_C_Z = 4608
P_A_SMALL = 5120
P_B_SMALL = 5376
P_C_SMALL = 5504
P_WIDTH = 5632

VMEM_LIMIT = 56 * 1024 * 1024


def _cparams(sem):
    return pltpu.CompilerParams(dimension_semantics=sem, vmem_limit_bytes=VMEM_LIMIT)


def _dot(a, b):
    return jnp.dot(a, b, preferred_element_type=F32)


def _dot_nt(a, b):
    return lax.dot_general(a, b, (((1,), (1,)), ((), ())), preferred_element_type=F32)


def _dot_tn(a, b):
    return lax.dot_general(a, b, (((0,), (0,)), ((), ())), preferred_element_type=F32)


def _bf(x):
    return x.astype(BF16)


def _layer_norm(x, g, b):
    mu = jnp.mean(x, -1, keepdims=True)
    xc = x - mu
    var = jnp.mean(xc * xc, -1, keepdims=True)
    return xc * lax.rsqrt(var + NORM_EPS) * g + b


def _softplus(x):
    return jnp.maximum(x, 0.0) + jnp.log1p(jnp.exp(-jnp.abs(x)))


def _sigmoid(x):
    return jax.nn.sigmoid(x)


def _silu(x):
    return x * jax.nn.sigmoid(x)


def _iota(shape, axis):
    return lax.broadcasted_iota(jnp.int32, shape, axis)


def _mask_bf(mask):
    return mask.astype(F32).astype(BF16)


def _block_mask(rows, cols, rblk, cblk):
    return (_iota((rows, cols), 0) // rblk) == (_iota((rows, cols), 1) // cblk)


def _bd(y, mask_bf, reps):
    return jnp.concatenate([y] * reps, axis=0) * mask_bf


def _pieces(x, count):
    out = []
    for _ in range(count - 1):
        p = _bf(x)
        out.append(p)
        x = x - p.astype(F32)
    out.append(_bf(x))
    return out


def _exact_rows(x, w_bf, count=3):
    n = x.shape[0]
    s = _dot(jnp.concatenate(_pieces(x, count), axis=0), w_bf)
    return sum(s[i * n:(i + 1) * n] for i in range(count))


def _exact_cols(w_bf, x, count=3):
    c = x.shape[1]
    s = _dot(w_bf, jnp.concatenate(_pieces(x, count), axis=1))
    return sum(s[:, i * c:(i + 1) * c] for i in range(count))


def _seg_sum(x, ones_bf):
    return jnp.concatenate([_exact_rows(x[:, i:i + HALF], ones_bf, 1) for i in range(0, x.shape[1], HALF)],
                           axis=1)


def _chunk_masks(n, width):
    return _iota((n, width), 0), _iota((n, width), 1) % n


def _tri_inv(ms, t, s, bdmask_bf):
    reps = ms[0].shape[1] // ms[0].shape[0]

    def mm(a_list, b_list):
        return [_dot(_bf(a), _bd(_bf(b), bdmask_bf, reps)) for a, b in zip(a_list, b_list)]

    def add(a_list, b_list):
        return [a + b for a, b in zip(a_list, b_list)]

    def sub(a_list, b_list):
        return [a - b for a, b in zip(a_list, b_list)]

    n = ms[0].shape[0]
    eye = (t == s).astype(F32)
    same16 = (t // 16) == (s // 16)
    same32 = (t // 32) == (s // 32)
    m16 = [jnp.where(same16, m, 0.0) for m in ms]
    x = [eye - m for m in m16]
    p = mm(m16, m16)
    for _ in range(2):
        xp = mm([jnp.concatenate([a, b], axis=0) for a, b in zip(x, p)], p)
        x = [a + b[0:n] for a, b in zip(x, xp)]
        p = [b[n:2 * n] for b in xp]
    x = add(x, mm(x, p))
    o32 = [jnp.where(same32 & jnp.logical_not(same16), m, 0.0) for m in ms]
    x = sub(x, mm(x, mm(o32, x)))
    o64 = [jnp.where(same32, 0.0, m) for m in ms]
    x = sub(x, mm(x, mm(o64, x)))
    return x


def _ln_kernel(x_ref, g_ref, b_ref, o_ref, ob_ref):
    y = _layer_norm(x_ref[...], g_ref[...], b_ref[...])
    o_ref[...] = y
    ob_ref[...] = y.astype(BF16)


def _ln_in(x, g, b, tm=256):
    m, d = x.shape
    return pl.pallas_call(
        _ln_kernel,
        grid=(m // tm,),
        in_specs=[pl.BlockSpec((tm, d), lambda i: (i, 0)),
                  pl.BlockSpec((1, d), lambda i: (0, 0)),
                  pl.BlockSpec((1, d), lambda i: (0, 0))],
        out_specs=[pl.BlockSpec((tm, d), lambda i: (i, 0)),
                   pl.BlockSpec((tm, d), lambda i: (i, 0))],
        out_shape=[jax.ShapeDtypeStruct((m, d), F32), jax.ShapeDtypeStruct((m, d), BF16)],
        compiler_params=_cparams(("parallel",)),
        name="ln_in",
    )(x, g.reshape(1, d), b.reshape(1, d))


def _mm_kernel(a_ref, w_ref, o_ref):
    o_ref[...] = _dot(a_ref[...], w_ref[...]).astype(o_ref.dtype)


def _matmul(a, w, layer, tm, tn, out_dtype, name):
    m, k = a.shape
    n = w.shape[2]
    tm = min(tm, m)
    return pl.pallas_call(
        _mm_kernel,
        grid=(n // tn, m // tm),
        in_specs=[pl.BlockSpec((tm, k), lambda j, i: (i, 0)),
                  pl.BlockSpec((None, k, tn), lambda j, i: (layer, 0, j))],
        out_specs=pl.BlockSpec((tm, tn), lambda j, i: (i, j)),
        out_shape=jax.ShapeDtypeStruct((m, n), out_dtype),
        compiler_params=_cparams(("parallel", "parallel")),
        name=name,
    )(a, w)


def _mixer_call(kern, p, blocks, params, scratch, name):
    bsz, seq, _ = p.shape
    n = CHUNK
    full = lambda arr: pl.BlockSpec(arr.shape, lambda c: (0,) * arr.ndim)
    in_specs = [pl.BlockSpec((bsz, n, width), functools.partial(lambda c, blk: (0, c, blk), blk=col // width))
                for col, width in blocks]
    return pl.pallas_call(
        kern,
        grid=(seq // n,),
        in_specs=in_specs + [full(t) for t in params],
        out_specs=pl.BlockSpec((bsz, n, 512), lambda c: (0, c, 0)),
        out_shape=jax.ShapeDtypeStruct((bsz, seq, 512), BF16),
        scratch_shapes=scratch,
        compiler_params=_cparams(("arbitrary",)),
        name=name,
    )(*([p] * len(blocks)), *params)


def _rwkv_kernel(main_ref, small_ref, mu_main_ref, mu_small_ref, w0_ref, w2_ref, a0_ref, a2_ref, g2_ref,
                 kk_ref, ka_ref, rk_ref, lnw_ref, lnb_ref, o_ref, xbuf, state):
    c = pl.program_id(0)
    nb = main_ref.shape[0]
    n = CHUNK
    wm = 3 * A_WIDTH
    reps = HALF // A_DIM
    halves = A_WIDTH // HALF

    @pl.when(c == 0)
    def _():
        xbuf[:, 0:HALO, :] = jnp.zeros((nb, HALO, xbuf.shape[2]), F32)
        state[...] = jnp.zeros(state.shape, F32)

    bdmask = _block_mask(HALF, HALF, A_DIM, A_DIM)
    bdmask_bf = _mask_bf(bdmask)
    t, s = _chunk_masks(n, HALF)
    incl = t >= s
    strict = t > s

    x_main, x_small = [], []
    for i in range(nb):
        raw_main = main_ref[i]
        raw_small = small_ref[i]
        xbuf[i, HALO:HALO + n, 0:wm] = raw_main
        xbuf[i, HALO:HALO + n, wm:] = raw_small
        prev = xbuf[i, HALO - 1:HALO - 1 + n, :]
        xbuf[i, 0:HALO, :] = xbuf[i, n:n + HALO, :]
        x_main.append(raw_main + (prev[:, 0:wm] - raw_main) * mu_main_ref[...])
        x_small.append(raw_small + (prev[:, wm:] - raw_small) * mu_small_ref[...])
    x_main = jnp.concatenate(x_main, axis=0)
    x_small = jnp.concatenate(x_small, axis=0)
    r = x_main[:, 0:A_WIDTH]
    k = x_main[:, A_WIDTH:2 * A_WIDTH]
    v_all = x_main[:, 2 * A_WIDTH:3 * A_WIDTH]
    wlog = -_softplus(-(w0_ref[...] + _dot(_bf(jnp.tanh(x_small)), w2_ref[...]))) - 0.5
    logdec = -jnp.exp(wlog)
    a = _sigmoid(a0_ref[...] + _dot(_bf(x_small), a2_ref[...]))
    g_all = _dot(_bf(_sigmoid(x_small)), g2_ref[...])
    kkr = k * kk_ref[...]
    k = k * (1.0 + (a - 1.0) * ka_ref[...])
    sums = _seg_sum(jnp.concatenate([kkr * kkr, r * k * rk_ref[...]], axis=0), bdmask_bf)
    kk = kkr * lax.rsqrt(sums[0:nb * n] + L2_EPS)
    bonus_all = sums[nb * n:] * v_all
    seq_tri_bf = _mask_bf((_iota((nb * n, nb * n), 0) >= _iota((nb * n, nb * n), 1))
                          & _block_mask(nb * n, nb * n, n, n))
    cum_all = _exact_cols(seq_tri_bf, logdec)
    b_all = kk * a
    a_t_all = -kk * jnp.exp(cum_all - logdec)
    r_t_all = r * jnp.exp(cum_all)
    e_neg = jnp.exp(-cum_all)
    b_t_all = b_all * e_neg
    k_t_all = k * e_neg
    seqs = []
    for i in range(nb):
        rows = slice(i * n, (i + 1) * n)
        cum = cum_all[rows]
        e_rest = jnp.exp(cum[n - 1:n, :] - cum)
        seqs.append(dict(
            v=v_all[rows], g=g_all[rows], bonus=bonus_all[rows],
            a_t=a_t_all[rows], r_t=r_t_all[rows], b_t=b_t_all[rows], k_t=k_t_all[rows],
            b_c=b_all[rows] * e_rest, k_c=k[rows] * e_rest,
            e_last=jnp.exp(cum[n - 1:n, :])))

    chains = [(i, hf) for i in range(nb) for hf in range(halves)]
    col = lambda name, i, hf: seqs[i][name][:, hf * HALF:(hf + 1) * HALF]
    s_old = [state[i, hf] for i, hf in chains]
    v_h = [col("v", i, hf) for i, hf in chains]
    bdv = [_bd(_bf(x), bdmask_bf, reps) for x in v_h]
    pr = [_dot_nt(_bf(jnp.concatenate([col("a_t", i, hf), col("r_t", i, hf)], axis=0)),
                  jnp.concatenate([_bd(_bf(col("b_t", i, hf)), bdmask_bf, reps),
                                   _bd(_bf(col("k_t", i, hf)), bdmask_bf, reps), _bf(s0)], axis=0))
          for (i, hf), s0 in zip(chains, s_old)]
    ab = [jnp.where(strict, x[0:n, 0:HALF], 0.0) for x in pr]
    rhs_u = [x[0:n, 2 * HALF:] + _dot(_bf(jnp.where(strict, x[0:n, HALF:2 * HALF], 0.0)), y)
             for x, y in zip(pr, bdv)]
    tinv = _tri_inv([-x for x in ab], t, s, bdmask_bf)
    u = [_dot(_bf(x), _bd(_bf(y), bdmask_bf, reps)) for x, y in zip(tinv, rhs_u)]
    o = [x[n:2 * n, 2 * HALF:] + _dot(
            _bf(jnp.concatenate([jnp.where(incl, x[n:2 * n, 0:HALF], 0.0),
                                 jnp.where(incl, x[n:2 * n, HALF:2 * HALF], 0.0)], axis=1)),
            jnp.concatenate([_bd(_bf(y), bdmask_bf, reps), z], axis=0))
         for x, y, z in zip(pr, u, bdv)]
    upd = [_dot_tn(_bf(jnp.concatenate([y, z], axis=0)),
                   _bf(jnp.concatenate([col("b_c", i, hf), col("k_c", i, hf)], axis=0)))
           for (i, hf), y, z in zip(chains, u, v_h)]
    for (i, hf), s0, x in zip(chains, s_old, upd):
        state[i, hf] = s0 * col("e_last", i, hf) + jnp.where(bdmask, x, 0.0)
    o_all = jnp.concatenate([jnp.concatenate(o[i * halves:(i + 1) * halves], axis=1) for i in range(nb)], axis=0)
    oc = o_all - _seg_sum(o_all, bdmask_bf) * (1.0 / A_DIM)
    var = _seg_sum(oc * oc, bdmask_bf) * (1.0 / A_DIM)
    out = ((oc * lax.rsqrt(var + A_GN_EPS)) * lnw_ref[...] + lnb_ref[...] + bonus_all) * g_all
    for i in range(nb):
        o_ref[i] = out[i * n:(i + 1) * n].astype(o_ref.dtype)


def _rwkv(p, params):
    bsz = p.shape[0]
    return _mixer_call(
        _rwkv_kernel, p, [(P_A_MAIN, 3 * A_WIDTH), (P_A_SMALL, 256)], params,
        [pltpu.VMEM((bsz, CHUNK + HALO, 3 * A_WIDTH + 256), F32),
         pltpu.VMEM((bsz, A_WIDTH // HALF, HALF, HALF), F32)], "rwkv7")


def _gla_kernel(main_ref, small_ref, gkw_ref, gkb_ref, nw_ref, o_ref, state):
    c = pl.program_id(0)
    nb = main_ref.shape[0]
    n = CHUNK

    @pl.when(c == 0)
    def _():
        state[...] = jnp.zeros(state.shape, F32)

    t, s = _chunk_masks(n, B_HEADS * n)
    incl = t >= s
    tri_bf = _mask_bf(_iota((n, n), 0) >= _iota((n, n), 1))
    kmask_bf = _mask_bf(_block_mask(B_HEADS * n, B_KEY_WIDTH, n, B_KEY))
    vmask_bf = _mask_bf(_block_mask(B_HEADS * n, B_WIDTH, n, B_VAL))
    smask = _block_mask(B_WIDTH, B_KEY_WIDTH, B_VAL, B_KEY)
    ones_bf = _mask_bf(_block_mask(HALF, HALF, B_VAL, B_VAL))

    qd, kd, k_c, e_last, v, g = [], [], [], [], [], []
    for i in range(nb):
        x = main_ref[i]
        q = x[:, 0:B_KEY_WIDTH] * (B_KEY ** -0.5)
        k = x[:, B_KEY_WIDTH:2 * B_KEY_WIDTH]
        v.append(x[:, 2 * B_KEY_WIDTH:2 * B_KEY_WIDTH + B_WIDTH])
        g.append(x[:, 2 * B_KEY_WIDTH + B_WIDTH:])
        gk = -_softplus(-(_dot(_bf(small_ref[i]), gkw_ref[...]) + gkb_ref[...])) / B_GATE_NORMALIZER
        bcum = _exact_cols(tri_bf, gk)
        qd.append(q * jnp.exp(bcum))
        kd.append(k * jnp.exp(-bcum))
        k_c.append(k * jnp.exp(bcum[n - 1:n, :] - bcum))
        e_last.append(jnp.exp(bcum[n - 1:n, :]))
    s_old = [state[i] for i in range(nb)]
    att = [jnp.where(incl, _dot_nt(_bf(x), _bd(_bf(y), kmask_bf, B_HEADS)), 0.0) for x, y in zip(qd, kd)]
    o = [_dot(_bf(x), _bd(_bf(y), vmask_bf, B_HEADS)) + _dot_nt(_bf(z), _bf(s0))
         for x, y, z, s0 in zip(att, v, qd, s_old)]
    upd = [_dot_tn(_bf(x), _bf(y)) for x, y in zip(v, k_c)]
    for i in range(nb):
        state[i] = s_old[i] * e_last[i] + jnp.where(smask, upd[i], 0.0)
        oi = o[i] * lax.rsqrt(_seg_sum(o[i] * o[i], ones_bf) * (1.0 / B_VAL) + NORM_EPS) * nw_ref[...]
        o_ref[i] = (oi * _silu(g[i])).astype(o_ref.dtype)


def _gla(p, params):
    bsz = p.shape[0]
    return _mixer_call(_gla_kernel, p, [(P_B_MAIN, 1536), (P_B_SMALL, SMALL)], params,
                       [pltpu.VMEM((bsz, B_WIDTH, B_KEY_WIDTH), F32)], "gla")


def _gdn_kernel(qkv_ref, z_ref, small_ref, cw_ref, alog_ref, dtb_ref, nw_ref, o_ref, xbuf, state):
    c = pl.program_id(0)
    nb = qkv_ref.shape[0]
    n = CHUNK
    hn = C_HEADS * n
    halves = C_WIDTH // HALF

    @pl.when(c == 0)
    def _():
        xbuf[:, 0:HALO, :] = jnp.zeros((nb, HALO, xbuf.shape[2]), F32)
        state[...] = jnp.zeros(state.shape, F32)

    smask = _block_mask(HALF, HALF, C_DIM, C_DIM)
    ones_bf = _mask_bf(smask)
    t, s = _chunk_masks(n, hn)
    incl = t >= s
    strict = t > s
    bdmask_bf = _mask_bf(_block_mask(hn, hn, n, n))
    dmask_bf = _mask_bf(_block_mask(hn, C_WIDTH, n, C_DIM))
    lane = _iota((SMALL, C_WIDTH + hn), 1)
    head_of = jnp.where(lane < C_WIDTH, lane // C_DIM, (lane - C_WIDTH) // n)
    spread_g = _mask_bf(_iota((SMALL, C_WIDTH + hn), 0) == head_of)
    spread_b = _mask_bf(_iota((SMALL, C_WIDTH), 0) == C_HEADS + _iota((SMALL, C_WIDTH), 1) // C_DIM)

    seqs = []
    qkv = []
    for i in range(nb):
        raw = qkv_ref[i]
        xbuf[i, HALO:HALO + n, :] = raw
        y = raw * cw_ref[C_CONV - 1:C_CONV, :]
        for j in range(1, C_CONV):
            y = y + xbuf[i, HALO - j:HALO - j + n, :] * cw_ref[C_CONV - 1 - j:C_CONV - j, :]
        xbuf[i, 0:HALO, :] = xbuf[i, n:n + HALO, :]
        qkv.append(_silu(y))
    qkv = jnp.concatenate(qkv, axis=0)
    q = qkv[:, 0:C_WIDTH]
    k = qkv[:, C_WIDTH:2 * C_WIDTH]
    v = qkv[:, 2 * C_WIDTH:]
    sums = _seg_sum(jnp.concatenate([q * q, k * k], axis=0), ones_bf)
    q = q * (lax.rsqrt(sums[0:nb * n] + L2_EPS) * (C_DIM ** -0.5))
    k = k * lax.rsqrt(sums[nb * n:] + L2_EPS)
    small = jnp.concatenate([small_ref[i] for i in range(nb)], axis=0)
    g_all = -jnp.exp(alog_ref[...]) * _softplus(small + dtb_ref[...])
    seq_tri_bf = _mask_bf((_iota((nb * n, nb * n), 0) >= _iota((nb * n, nb * n), 1))
                          & _block_mask(nb * n, nb * n, n, n))
    gam = _exact_rows(_exact_cols(seq_tri_bf, g_all), spread_g)
    gam_w_all = gam[:, 0:C_WIDTH]
    gam_n_all = gam[:, C_WIDTH:]
    beta = _exact_rows(_sigmoid(small), spread_b)
    eg = jnp.exp(gam_w_all)
    kb = k * beta
    vb_all, kbe_all, qg_all = v * beta, kb * eg, q * eg
    for i in range(nb):
        rows = slice(i * n, (i + 1) * n)
        gam_w, gam_n = gam_w_all[rows], gam_n_all[rows]
        grow = jnp.sum(jnp.where(t == s, gam_n, 0.0), axis=0, keepdims=True)
        dec = jnp.exp(jnp.where(incl, gam_n - grow, 0.0))
        glast = gam_w[n - 1:n, :]
        seqs.append(dict(q=q[rows], k=k[rows], kb=kb[rows], dec=dec, vb=vb_all[rows], kbe=kbe_all[rows],
                         qg=qg_all[rows], kg=k[rows] * jnp.exp(glast - gam_w), e_last=jnp.exp(glast)))

    pr = [_dot_nt(_bf(jnp.concatenate([d["kb"], d["q"]], axis=0)), _bd(_bf(d["k"]), dmask_bf, C_HEADS))
          for d in seqs]
    m = [jnp.where(strict, x[0:n] * d["dec"], 0.0) for x, d in zip(pr, seqs)]
    qk = [jnp.where(incl, x[n:2 * n] * d["dec"], 0.0) for x, d in zip(pr, seqs)]
    tinv = _tri_inv(m, t, s, bdmask_bf)
    uw = [_dot(_bf(x), jnp.concatenate([_bd(_bf(d["vb"]), dmask_bf, C_HEADS),
                                        _bd(_bf(d["kbe"]), dmask_bf, C_HEADS)], axis=1))
          for x, d in zip(tinv, seqs)]
    chains = [(i, hf) for i in range(nb) for hf in range(halves)]
    s_old = [state[i, hf] for i, hf in chains]
    ws = [_dot(_bf(jnp.concatenate([uw[i][:, C_WIDTH + hf * HALF:C_WIDTH + (hf + 1) * HALF],
                                    seqs[i]["qg"][:, hf * HALF:(hf + 1) * HALF]], axis=0)), _bf(s0))
          for (i, hf), s0 in zip(chains, s_old)]
    vn = [uw[i][:, hf * HALF:(hf + 1) * HALF] - x[0:n] for (i, hf), x in zip(chains, ws)]
    upd = [_dot_tn(_bf(seqs[i]["kg"][:, hf * HALF:(hf + 1) * HALF]), _bf(x)) for (i, hf), x in zip(chains, vn)]
    for (i, hf), s0, x in zip(chains, s_old, upd):
        state[i, hf] = s0 * seqs[i]["e_last"][:, hf * HALF:(hf + 1) * HALF] + jnp.where(smask, x, 0.0)
    outs = []
    for i in range(nb):
        v_new = jnp.concatenate(vn[i * halves:(i + 1) * halves], axis=1)
        o_inter = jnp.concatenate([x[n:2 * n] for x in ws[i * halves:(i + 1) * halves]], axis=1)
        outs.append(o_inter + _dot(_bf(qk[i]), _bd(_bf(v_new), dmask_bf, C_HEADS)))
    o_all = jnp.concatenate(outs, axis=0)
    o_all = o_all * lax.rsqrt(_seg_sum(o_all * o_all, ones_bf) * (1.0 / C_DIM) + NORM_EPS) * nw_ref[...]
    for i in range(nb):
        o_ref[i] = (o_all[i * n:(i + 1) * n] * _silu(z_ref[i])).astype(o_ref.dtype)


def _gdn(p, params):
    bsz = p.shape[0]
    return _mixer_call(
        _gdn_kernel, p, [(P_C_QKV, 3 * C_WIDTH), (P_C_Z, C_WIDTH), (P_C_SMALL, SMALL)], params,
        [pltpu.VMEM((bsz, CHUNK + HALO, 3 * C_WIDTH), F32),
         pltpu.VMEM((bsz, C_WIDTH // HALF, HALF, HALF), F32)], "gdn")


def _merge_kernel(h_ref, ya_ref, yb_ref, yc_ref, wg_ref, wb_ref, o_ref):
    h = h_ref[...]
    acc = None
    for i, y_ref in enumerate((ya_ref, yb_ref, yc_ref)):
        t = _sigmoid(_dot(h, wg_ref[i])) * _dot(y_ref[...], wb_ref[i])
        acc = t if acc is None else acc + t
    o_ref[...] = acc.astype(o_ref.dtype)


def _merge(hb, ya, yb, yc, wg, wb, layer, tm=1024, tn=512):
    m, d = hb.shape
    bw = ya.shape[1]
    tm = min(tm, m)
    yspec = pl.BlockSpec((tm, bw), lambda i, j: (i, 0))
    return pl.pallas_call(
        _merge_kernel,
        grid=(m // tm, d // tn),
        in_specs=[pl.BlockSpec((tm, d), lambda i, j: (i, 0)), yspec, yspec, yspec,
                  pl.BlockSpec((None, 3, d, tn), lambda i, j: (layer, 0, 0, j)),
                  pl.BlockSpec((None, 3, bw, tn), lambda i, j: (layer, 0, 0, j))],
        out_specs=pl.BlockSpec((tm, tn), lambda i, j: (i, j)),
        out_shape=jax.ShapeDtypeStruct((m, d), BF16),
        compiler_params=_cparams(("parallel", "parallel")),
        name="merge",
    )(hb, ya, yb, yc, wg, wb)


def _proj_ln_kernel(a_ref, w_ref, h_ref, g_ref, b_ref, o_ref, ob_ref):
    y = _layer_norm(DEEPNORM_ALPHA * h_ref[...] + _dot(a_ref[...], w_ref[...]), g_ref[...], b_ref[...])
    o_ref[...] = y
    ob_ref[...] = y.astype(BF16)


def _proj_ln(a, w, layer, h, g, b, tm, name):
    m, k = a.shape
    d = w.shape[2]
    tm = min(tm, m)
    return pl.pallas_call(
        _proj_ln_kernel,
        grid=(m // tm,),
        in_specs=[pl.BlockSpec((tm, k), lambda i: (i, 0)),
                  pl.BlockSpec((None, k, d), lambda i: (layer, 0, 0), pipeline_mode=pl.Buffered(1)),
                  pl.BlockSpec((tm, d), lambda i: (i, 0)),
                  pl.BlockSpec((1, d), lambda i: (0, 0)),
                  pl.BlockSpec((1, d), lambda i: (0, 0))],
        out_specs=[pl.BlockSpec((tm, d), lambda i: (i, 0)),
                   pl.BlockSpec((tm, d), lambda i: (i, 0))],
        out_shape=[jax.ShapeDtypeStruct((m, d), F32), jax.ShapeDtypeStruct((m, d), BF16)],
        compiler_params=_cparams(("parallel",)),
        name=name,
    )(a, w, h, g.reshape(1, d), b.reshape(1, d))


def _ffn_up_kernel(a_ref, wg_ref, wu_ref, cwg_ref, cwu_ref, bg_ref, bu_ref, o_ref, gcar, ucar, *, blocks_per_seq):
    i = pl.program_id(0)
    j = pl.program_id(1)
    tm = a_ref.shape[0]
    a = a_ref[...]
    first = (i % blocks_per_seq) == 0
    head_row = _iota((HALO, o_ref.shape[1]), 0)

    def shifted(u, tail, k):
        r = pltpu.roll(u, k, axis=0)
        head = jnp.where(head_row < k, pltpu.roll(tail, k, axis=0), r[0:HALO])
        return jnp.concatenate([head, r[HALO:]], axis=0)

    def conv(w_ref, cw_ref, b_ref, car):
        u = _dot(a, w_ref[...])
        tail = jnp.where(first, 0.0, car[j])
        car[j] = u[tm - HALO:tm, :]
        return (u * cw_ref[2:3, :] + shifted(u, tail, 1) * cw_ref[1:2, :]
                + shifted(u, tail, 2) * cw_ref[0:1, :] + b_ref[...])

    gate = conv(wg_ref, cwg_ref, bg_ref, gcar)
    up = conv(wu_ref, cwu_ref, bu_ref, ucar)
    o_ref[...] = (_silu(gate) * up).astype(o_ref.dtype)


def _ffn_up(hb, w_up, layer, conv_w, conv_b, seq, tm=1024, tn=512):
    m, d = hb.shape
    nf = w_up.shape[2] // 2
    nj = nf // tn
    tm = min(tm, seq)
    kern = functools.partial(_ffn_up_kernel, blocks_per_seq=seq // tm)
    return pl.pallas_call(
        kern,
        grid=(m // tm, nj),
        in_specs=[pl.BlockSpec((tm, d), lambda i, j: (i, 0)),
                  pl.BlockSpec((None, d, tn), lambda i, j: (layer, 0, j)),
                  pl.BlockSpec((None, d, tn), lambda i, j: (layer, 0, j + nj)),
                  pl.BlockSpec((FFN_CONV, tn), lambda i, j: (0, j)),
                  pl.BlockSpec((FFN_CONV, tn), lambda i, j: (0, j + nj)),
                  pl.BlockSpec((1, tn), lambda i, j: (0, j)),
                  pl.BlockSpec((1, tn), lambda i, j: (0, j + nj))],
        out_specs=pl.BlockSpec((tm, tn), lambda i, j: (i, j)),
        out_shape=jax.ShapeDtypeStruct((m, nf), BF16),
        scratch_shapes=[pltpu.VMEM((nj, HALO, tn), F32), pltpu.VMEM((nj, HALO, tn), F32)],
        compiler_params=_cparams(("arbitrary", "arbitrary")),
        name="ffn_up",
    )(hb, w_up, w_up, conv_w, conv_w, conv_b, conv_b)


def _pad_cols(t, width):
    return jnp.pad(t, ((0, 0), (0, width - t.shape[1])))


def _pad_rows(t, before, total):
    return jnp.pad(t, ((before, total - before - t.shape[0]), (0, 0)))


def _relayout_w_in(w):
    a_in = 3 * A_WIDTH + A_DECAY_LORA + A_ICLR_LORA + A_GATE_LORA
    b_in = 2 * B_KEY_WIDTH + B_WIDTH + B_GATE_LORA + B_WIDTH
    wa, wb, wc = w[..., :a_in], w[..., a_in:a_in + b_in], w[..., a_in + b_in:]
    b_qkv = wb[..., :2 * B_KEY_WIDTH + B_WIDTH]
    b_gk = wb[..., 2 * B_KEY_WIDTH + B_WIDTH:2 * B_KEY_WIDTH + B_WIDTH + B_GATE_LORA]
    b_g = wb[..., 2 * B_KEY_WIDTH + B_WIDTH + B_GATE_LORA:]
    c_qkv = wc[..., :3 * C_WIDTH]
    c_ab = wc[..., 3 * C_WIDTH:3 * C_WIDTH + 2 * C_HEADS]
    c_z = wc[..., 3 * C_WIDTH + 2 * C_HEADS:]

    def pad(t, width):
        return jnp.pad(t, [(0, 0)] * (t.ndim - 1) + [(0, width - t.shape[-1])])

    return jnp.concatenate([
        wa[..., :3 * A_WIDTH], b_qkv, b_g, c_qkv, c_z,
        pad(wa[..., 3 * A_WIDTH:], 256), pad(b_gk, SMALL), pad(c_ab, SMALL)], axis=-1)


def kernel(x, ln_in_g, ln_in_b, w_in, mu_a, a_w0, a_w2, a_a0, a_a2, a_g2, a_kk, a_ka, a_rk, a_lnx_w, a_lnx_b,
           b_gk_w2, b_gk_b, b_norm_w, c_conv_w, c_a_log, c_dt_bias, c_norm_w, w_gate, w_branch, w_out,
           ln1_g, ln1_b, w_up, ffn_conv_w, ffn_conv_b, w_down, ln2_g, ln2_b):
    bsz, seq, d = x.shape
    m = bsz * seq
    assert seq % CHUNK == 0
    h, hb = _ln_in(x.reshape(m, d), ln_in_g, ln_in_b)
    r1 = lambda t: t.reshape(1, -1)
    w_in_b = _relayout_w_in(w_in.astype(BF16))
    w_gate_b, w_branch_b, w_out_b = w_gate.astype(BF16), w_branch.astype(BF16), w_out.astype(BF16)
    w_up_b, w_down_b = w_up.astype(BF16), w_down.astype(BF16)
    for l in range(DEPTH):
        p = _matmul(hb, w_in_b, l, 512, P_WIDTH // 2, F32, "in_proj")
        p = p.reshape(bsz, seq, P_WIDTH)
        lo = A_DECAY_LORA
        ya = _rwkv(p, (
            r1(mu_a[l, :3 * A_WIDTH]), _pad_cols(r1(mu_a[l, 3 * A_WIDTH:]), 256),
            r1(a_w0[l]), _pad_rows(a_w2[l], 0, 256).astype(BF16),
            r1(a_a0[l]), _pad_rows(a_a2[l], lo, 256).astype(BF16),
            _pad_rows(a_g2[l], lo + A_ICLR_LORA, 256).astype(BF16),
            r1(a_kk[l]), r1(a_ka[l]), r1(a_rk[l]), r1(a_lnx_w[l]), r1(a_lnx_b[l])))
        yb = _gla(p, (_pad_rows(b_gk_w2[l], 0, SMALL).astype(BF16), r1(b_gk_b[l]),
                      jnp.tile(r1(b_norm_w[l]), (1, B_HEADS))))
        yc = _gdn(p, (c_conv_w[l], _pad_cols(r1(c_a_log[l]), SMALL), _pad_cols(r1(c_dt_bias[l]), SMALL),
                      jnp.tile(r1(c_norm_w[l]), (1, C_HEADS))))
        merged = _merge(hb, ya.reshape(m, -1), yb.reshape(m, -1), yc.reshape(m, -1), w_gate_b, w_branch_b, l)
        h, hb = _proj_ln(merged, w_out_b, l, h, ln1_g[l], ln1_b[l], 512, "out_proj")
        act = _ffn_up(hb, w_up_b, l, ffn_conv_w[l], r1(ffn_conv_b[l]), seq)
        h, hb = _proj_ln(act, w_down_b, l, h, ln2_g[l], ln2_b[l], 256, "ffn_down")
    return h.reshape(bsz, seq, d)
```
